```python
import jax, jax.numpy as jnp
from jax import lax
import numpy as np

D_MODEL = 2048
BATCH = 4
SEQ = 2048
DEPTH = 1
DEC_BATCH = 128
DEC_SEQ = 1
PAST_LEN = 16384
PAGE_SIZE = 128

N_META = 16
N_HEADS = 12
QK_NOPE = 128
QK_ROPE = 64
V_DIM = 128
Q_RANK = 512
KV_RANK = 512
MLA_WIDTH = N_HEADS * V_DIM
POOL_WIDTH = D_MODEL - MLA_WIDTH
POOL_WINDOWS = (2, 4, 8, 16)
N_POOL_GROUPS = 4
POOL_GROUP = POOL_WIDTH // N_POOL_GROUPS
POOL_HIST = 15
D_FF = 4 * D_MODEL
IN_COLS = Q_RANK + KV_RANK + QK_ROPE + POOL_WIDTH
ROPE_THETA = 10000.0
NORM_EPS = 1e-6
Q_BLOCK = 128

kernel_name = 'hymba_mla_pool_decode_step'


def rmsnorm(x, g):
    xf = x.astype(jnp.float32)
    y = xf * lax.rsqrt(jnp.mean(xf * xf, axis=-1, keepdims=True) + NORM_EPS)
    return (y * g.astype(jnp.float32)).astype(x.dtype)


def rope_tables(pos):
    inv = ROPE_THETA ** (-jnp.arange(0, QK_ROPE, 2, dtype=jnp.float32) / QK_ROPE)
    ang = pos.astype(jnp.float32)[:, None] * inv[None, :]
    ang = jnp.concatenate([ang, ang], axis=-1)
    return jnp.cos(ang), jnp.sin(ang)


def apply_rope(x, cos, sin):
    x1, x2 = jnp.split(x, 2, axis=-1)
    rot = jnp.concatenate([-x2, x1], axis=-1)
    return (x * cos + rot * sin).astype(x.dtype)


def mixer_inputs(x, pos, g_attn, w_in, g_q, w_q_b, g_kv):
    b, s = x.shape[0], x.shape[1]
    h = rmsnorm(x, g_attn)
    proj = h @ w_in
    q_a, kv_a, kr, pool_in = jnp.split(
        proj, [Q_RANK, Q_RANK + KV_RANK, Q_RANK + KV_RANK + QK_ROPE], axis=-1)
    q = (rmsnorm(q_a, g_q) @ w_q_b).reshape(b, s, N_HEADS, QK_NOPE + QK_ROPE)
    q_nope, q_pe = q[..., :QK_NOPE], q[..., QK_NOPE:]
    cos, sin = rope_tables(pos)
    q_pe = apply_rope(q_pe, cos[:, None, :], sin[:, None, :])
    c_kv = rmsnorm(kv_a, g_kv)
    k_pe = apply_rope(kr, cos, sin)
    return q_nope, q_pe, c_kv, k_pe, pool_in


def mla_prompt(q_nope, q_pe, c_kv, k_pe, w_kv_b):
    b, L = c_kv.shape[0], c_kv.shape[1]
    kv = jnp.einsum('blr,rhd->blhd', c_kv, w_kv_b)
    k_nope, v = kv[..., :QK_NOPE], kv[..., QK_NOPE:]
    n_blk = -(-L // Q_BLOCK)
    pad = n_blk * Q_BLOCK - L
    qn = jnp.pad(q_nope, ((0, 0), (0, pad), (0, 0), (0, 0)))
    qn = qn.reshape(b, n_blk, Q_BLOCK, N_HEADS, QK_NOPE).transpose(1, 0, 2, 3, 4)
    qp = jnp.pad(q_pe, ((0, 0), (0, pad), (0, 0), (0, 0)))
    qp = qp.reshape(b, n_blk, Q_BLOCK, N_HEADS, QK_ROPE).transpose(1, 0, 2, 3, 4)
    key_pos = jnp.arange(L, dtype=jnp.int32)
    scale = (QK_NOPE + QK_ROPE) ** -0.5

    def block(args):
        qn_b, qp_b, i = args
        s = (jnp.einsum('bqhd,bkhd->bhqk', qn_b, k_nope)
             + jnp.einsum('bqhe,bke->bhqk', qp_b, k_pe)).astype(jnp.float32) * scale
        q_pos = i * Q_BLOCK + jnp.arange(Q_BLOCK, dtype=jnp.int32)
        s = jnp.where(key_pos[None, :] <= q_pos[:, None], s, -jnp.inf)
        p = jax.nn.softmax(s, axis=-1).astype(v.dtype)
        return jnp.einsum('bhqk,bkhv->bqhv', p, v)

    o = lax.map(block, (qn, qp, jnp.arange(n_blk, dtype=jnp.int32)))
    return o.transpose(1, 0, 2, 3, 4).reshape(b, n_blk * Q_BLOCK, MLA_WIDTH)[:, :L]


def mla_sample(q_nope, q_pe, c_kv, k_pe, cache_lat, cache_kr, page_table, w_kv_b):
    db, s1 = c_kv.shape[0], c_kv.shape[1]
    w_uk, w_uv = w_kv_b[..., :QK_NOPE], w_kv_b[..., QK_NOPE:]
    q_lat = jnp.einsum('bshn,rhn->bshr', q_nope, w_uk)
    scale = (QK_NOPE + QK_ROPE) ** -0.5

    def page_step(carry, phys):
        m, l, acc = carry
        c = cache_lat[phys]
        kp = cache_kr[phys]
        s = (jnp.einsum('bshr,bpr->bshp', q_lat, c)
             + jnp.einsum('bshe,bpe->bshp', q_pe, kp)).astype(jnp.float32) * scale
        m_new = jnp.maximum(m, jnp.max(s, axis=-1))
        corr = jnp.exp(m - m_new)
        p = jnp.exp(s - m_new[..., None])
        l = l * corr + jnp.sum(p, axis=-1)
        acc = acc * corr[..., None] + jnp.einsum('bshp,bpr->bshr', p, c.astype(jnp.float32))
        return (m_new, l, acc), None

    init = (jnp.full((db, s1, N_HEADS), -jnp.inf, jnp.float32),
            jnp.zeros((db, s1, N_HEADS), jnp.float32),
            jnp.zeros((db, s1, N_HEADS, KV_RANK), jnp.float32))
    (m, l, acc), _ = lax.scan(page_step, init, page_table.T)

    s = (jnp.einsum('bshr,btr->bsht', q_lat, c_kv)
         + jnp.einsum('bshe,bte->bsht', q_pe, k_pe)).astype(jnp.float32) * scale
    causal = jnp.arange(s1)[:, None] >= jnp.arange(s1)[None, :]
    s = jnp.where(causal[None, :, None, :], s, -jnp.inf)
    m_new = jnp.maximum(m, jnp.max(s, axis=-1))
    corr = jnp.exp(m - m_new)
    p = jnp.exp(s - m_new[..., None])
    l = l * corr + jnp.sum(p, axis=-1)
    acc = acc * corr[..., None] + jnp.einsum('bsht,btr->bshr', p, c_kv.astype(jnp.float32))
    lat = (acc / l[..., None]).astype(w_uv.dtype)
    return jnp.einsum('bshr,rhv->bshv', lat, w_uv).reshape(db, s1, MLA_WIDTH)


def pool_mixer(buf, start_pos, n_out, w_pool, pool_scale):
    b, T = buf.shape[0], buf.shape[1]
    xf = buf.astype(jnp.float32).reshape(b, T, N_POOL_GROUPS, POOL_GROUP)
    cs = jnp.cumsum(xf, axis=1)
    pos = start_pos + jnp.arange(T, dtype=jnp.int32)
    outs = []
    for g, w in enumerate(POOL_WINDOWS):
        csg = cs[:, :, g]
        prev = jnp.pad(csg, ((0, 0), (w, 0), (0, 0)))[:, :T]
        cnt = jnp.minimum(w, pos + 1).astype(jnp.float32)
        outs.append((csg - prev) / cnt[None, :, None])
    pooled = jnp.stack(outs, axis=2)
    mixed = (pooled - xf)[:, T - n_out:].astype(w_pool.dtype)
    y = jnp.einsum('btgc,gcd->btgd', mixed, w_pool).reshape(b, n_out, POOL_WIDTH)
    return y * pool_scale


def layer_out(x, mla_o, pool_o, w_o, g_mlp, w_up, w_down):
    h = x + jnp.concatenate([mla_o, pool_o], axis=-1) @ w_o
    z = rmsnorm(h, g_mlp) @ w_up
    return h + jnp.square(jax.nn.relu(z)) @ w_down


def setup_inputs(seed: int = 0) -> dict:
    key = jax.random.key(seed)
    ks = jax.random.split(key, 24)
    n_pages = PAST_LEN // PAGE_SIZE
    n_used = DEC_BATCH * n_pages
    n_phys = (5 * n_used) // 4
    f32 = jnp.float32
    nrm = lambda k, shp, sc=1.0: jax.random.normal(k, shp, f32) * sc
    page_table = jax.random.permutation(ks[5], n_phys)[:n_used].reshape(DEC_BATCH, n_pages).astype(jnp.int32)
    return {
        'x_prompt': nrm(ks[0], (BATCH, SEQ, D_MODEL)),
        'x_sample': nrm(ks[1], (DEC_BATCH, DEC_SEQ, D_MODEL)),
        'cache_kv_latent': nrm(ks[2], (DEPTH, n_phys, PAGE_SIZE, KV_RANK)),
        'cache_k_rope': nrm(ks[3], (DEPTH, n_phys, PAGE_SIZE, QK_ROPE)),
        'state_pool': nrm(ks[4], (DEPTH, DEC_BATCH, POOL_HIST, POOL_WIDTH)),
        'page_table': page_table,
        'meta_tokens': nrm(ks[6], (N_META, D_MODEL)),
        'g_attn': 1.0 + nrm(ks[7], (DEPTH, D_MODEL), 0.02),
        'w_in': nrm(ks[8], (DEPTH, D_MODEL, IN_COLS), D_MODEL ** -0.5),
        'g_q': 1.0 + nrm(ks[9], (DEPTH, Q_RANK), 0.02),
        'w_q_b': nrm(ks[10], (DEPTH, Q_RANK, N_HEADS * (QK_NOPE + QK_ROPE)), Q_RANK ** -0.5),
        'g_kv': 1.0 + nrm(ks[11], (DEPTH, KV_RANK), 0.02),
        'w_kv_b': nrm(ks[12], (DEPTH, KV_RANK, N_HEADS, QK_NOPE + V_DIM), KV_RANK ** -0.5),
        'w_pool': nrm(ks[13], (DEPTH, N_POOL_GROUPS, POOL_GROUP, POOL_GROUP), POOL_GROUP ** -0.5),
        'pool_scale': 1.0 + nrm(ks[14], (DEPTH, POOL_WIDTH), 0.1),
        'w_o': nrm(ks[15], (DEPTH, D_MODEL, D_MODEL), D_MODEL ** -0.5),
        'g_mlp': 1.0 + nrm(ks[16], (DEPTH, D_MODEL), 0.02),
        'w_up': nrm(ks[17], (DEPTH, D_MODEL, D_FF), D_MODEL ** -0.5),
        'w_down': nrm(ks[18], (DEPTH, D_FF, D_MODEL), D_FF ** -0.5),
        'g_final': 1.0 + nrm(ks[19], (D_MODEL,), 0.02),
    }


def reference(x_prompt, x_sample, cache_kv_latent, cache_k_rope, state_pool, page_table,
              meta_tokens, g_attn, w_in, g_q, w_q_b, g_kv, w_kv_b, w_pool, pool_scale,
              w_o, g_mlp, w_up, w_down, g_final):
    b = x_prompt.shape[0]
    meta = jnp.broadcast_to(meta_tokens.astype(x_prompt.dtype)[None], (b, N_META, D_MODEL))
    hp = jnp.concatenate([meta, x_prompt], axis=1)
    L = hp.shape[1]
    s1 = x_sample.shape[1]
    pos_p = jnp.arange(L, dtype=jnp.int32)
    pos_s = PAST_LEN + jnp.arange(s1, dtype=jnp.int32)
    hs = x_sample
    lat_p, kr_p, pool_p, lat_s, kr_s, pool_s = [], [], [], [], [], []
    for layer in range(DEPTH):
        qn, qp, ckv, kpe, pin = mixer_inputs(hp, pos_p, g_attn[layer], w_in[layer],
                                             g_q[layer], w_q_b[layer], g_kv[layer])
        a_p = mla_prompt(qn, qp, ckv, kpe, w_kv_b[layer])
        o_p = pool_mixer(pin, 0, L, w_pool[layer], pool_scale[layer])
        lat_p.append(ckv)
        kr_p.append(kpe)
        pool_p.append(pin[:, L - POOL_HIST:])
        hp = layer_out(hp, a_p, o_p, w_o[layer], g_mlp[layer], w_up[layer], w_down[layer])
        qn, qp, ckv, kpe, pin = mixer_inputs(hs, pos_s, g_attn[layer], w_in[layer],
                                             g_q[layer], w_q_b[layer], g_kv[layer])
        a_s = mla_sample(qn, qp, ckv, kpe, cache_kv_latent[layer], cache_k_rope[layer],
                         page_table, w_kv_b[layer])
        buf = jnp.concatenate([state_pool[layer].astype(pin.dtype), pin], axis=1)
        o_s = pool_mixer(buf, PAST_LEN - POOL_HIST, s1, w_pool[layer], pool_scale[layer])
        lat_s.append(ckv)
        kr_s.append(kpe)
        pool_s.append(buf[:, buf.shape[1] - POOL_HIST:])
        hs = layer_out(hs, a_s, o_s, w_o[layer], g_mlp[layer], w_up[layer], w_down[layer])
    y_prompt = rmsnorm(hp, g_final)[:, N_META:]
    y_sample = rmsnorm(hs, g_final)
    return (y_prompt, y_sample, jnp.stack(lat_p), jnp.stack(kr_p), jnp.stack(pool_p),
            jnp.stack(lat_s), jnp.stack(kr_s), jnp.stack(pool_s))
```

```python
from functools import partial

import jax
import jax.numpy as jnp
from jax import lax
from jax.experimental import pallas as pl
from jax.experimental.pallas import tpu as pltpu

D_MODEL = 2048
N_META = 16
N_HEADS = 12
QK_NOPE = 128
QK_ROPE = 64
V_DIM = 128
Q_RANK = 512
KV_RANK = 512
MLA_WIDTH = N_HEADS * V_DIM
POOL_WIDTH = D_MODEL - MLA_WIDTH
POOL_WINDOWS = (2, 4, 8, 16)
POOL_GROUP = 128
POOL_HIST = 15
D_FF = 4 * D_MODEL
ROPE_THETA = 10000.0
NORM_EPS = 1e-6
PAST_LEN = 16384
PAGE_SIZE = 128

LANES = 128
HEAD_PAD = 256
HEADS_PAD = 16
QCAT = 640
IN_COLS_R = Q_RANK + KV_RANK + 2 * QK_ROPE + POOL_WIDTH
SCALE = (QK_NOPE + QK_ROPE) ** -0.5
VMEM_LIMIT = 56 * 1024 * 1024
PAGES_PER_STEP = 8

BF16 = jnp.bfloat16
F32 = jnp.float32


def _rms(x, g):
    return x * lax.rsqrt(jnp.mean(x * x, axis=-1, keepdims=True) + NORM_EPS) * g


def _rope128(x, c, s):
    return x * c + pltpu.roll(x, QK_ROPE, axis=1) * s


def _const_spec(shape):
    n = len(shape)
    return pl.BlockSpec(shape, lambda *_: (0,) * n, pipeline_mode=pl.Buffered(1))


def _mixer_in_kernel(x_ref, ga_ref, win_ref, gq_ref, wq_ref, gkv_ref, wk_ref, wv_ref,
                     c_ref, s_ref, lat_ref, kpe_ref, pool_ref, q_ref, k_ref, v_ref):
    h = _rms(x_ref[...], ga_ref[...]).astype(BF16)
    proj = jnp.dot(h, win_ref[...], preferred_element_type=F32)
    cos = c_ref[...]
    sin = s_ref[...]
    ckv = _rms(proj[:, Q_RANK:Q_RANK + KV_RANK], gkv_ref[...])
    lat_ref[...] = ckv
    kpe = _rope128(proj[:, 2 * Q_RANK:2 * Q_RANK + LANES], cos, sin)
    kpe_ref[...] = kpe
    pool_ref[...] = proj[:, 2 * Q_RANK + LANES:]
    kpe_b = kpe.astype(BF16)

    qn = _rms(proj[:, :Q_RANK], gq_ref[...]).astype(BF16)
    q = jnp.dot(qn, wq_ref[...], preferred_element_type=F32)
    for hh in range(N_HEADS):
        base = hh * HEAD_PAD
        q_ref[hh, :, :LANES] = (q[:, base:base + LANES] * SCALE).astype(BF16)
        pe = _rope128(q[:, base + LANES:base + HEAD_PAD], cos, sin) * SCALE
        q_ref[hh, :, LANES:] = pe.astype(BF16)

    cb = ckv.astype(BF16)
    kn = jnp.dot(cb, wk_ref[...], preferred_element_type=F32)
    vv = jnp.dot(cb, wv_ref[...], preferred_element_type=F32)
    for hh in range(N_HEADS):
        k_ref[hh, :, :LANES] = kn[:, hh * LANES:(hh + 1) * LANES].astype(BF16)
        k_ref[hh, :, LANES:] = kpe_b
        v_ref[hh] = vv[:, hh * LANES:(hh + 1) * LANES].astype(BF16)


def _mixer_in(x, cos, sin, ga, win, gq, wq, gkv, wk, wv, tm):
    t = x.shape[0]
    row = lambda w: pl.BlockSpec((tm, w), lambda i: (i, 0))
    head = lambda w: pl.BlockSpec((N_HEADS, tm, w), lambda i: (0, i, 0))
    return pl.pallas_call(
        _mixer_in_kernel,
        grid=(t // tm,),
        in_specs=[row(D_MODEL), _const_spec((1, D_MODEL)), _const_spec((D_MODEL, IN_COLS_R)),
                  _const_spec((1, Q_RANK)), _const_spec((Q_RANK, N_HEADS * HEAD_PAD)),
                  _const_spec((1, KV_RANK)), _const_spec((KV_RANK, MLA_WIDTH)),
                  _const_spec((KV_RANK, MLA_WIDTH)), row(LANES), row(LANES)],
        out_specs=[row(KV_RANK), row(LANES), row(POOL_WIDTH),
                   head(HEAD_PAD), head(HEAD_PAD), head(V_DIM)],
        out_shape=[jax.ShapeDtypeStruct((t, KV_RANK), F32),
                   jax.ShapeDtypeStruct((t, LANES), F32),
                   jax.ShapeDtypeStruct((t, POOL_WIDTH), F32),
                   jax.ShapeDtypeStruct((N_HEADS, t, HEAD_PAD), BF16),
                   jax.ShapeDtypeStruct((N_HEADS, t, HEAD_PAD), BF16),
                   jax.ShapeDtypeStruct((N_HEADS, t, V_DIM), BF16)],
        compiler_params=pltpu.CompilerParams(
            dimension_semantics=("arbitrary",), vmem_limit_bytes=VMEM_LIMIT),
        name="mixer_in",
    )(x, ga, win, gq, wq, gkv, wk, wv, cos, sin)


def _attn_kernel(q_ref, k_ref, v_ref, o_ref, *, tq):
    i = pl.program_id(2)
    q = q_ref[0]
    qpos = i * tq + lax.broadcasted_iota(jnp.int32, (tq, tq), 0)
    kidx = lax.broadcasted_iota(jnp.int32, (tq, tq), 1)

    def step(j, carry):
        m, l, acc = carry
        start = pl.multiple_of(j * tq, tq)
        kj = k_ref[0, pl.ds(start, tq), :]
        vj = v_ref[0, pl.ds(start, tq), :]
        s = lax.dot_general(q, kj, (((1,), (1,)), ((), ())), preferred_element_type=F32)
        s = jnp.where(j * tq + kidx <= qpos, s, -jnp.inf)
        m_new = jnp.maximum(m, jnp.max(s, axis=-1, keepdims=True))
        p = jnp.exp(s - m_new)
        corr = jnp.exp(m - m_new)
        l = l * corr + jnp.sum(p, axis=-1, keepdims=True)
        acc = acc * corr + jnp.dot(p.astype(BF16), vj, preferred_element_type=F32)
        return m_new, l, acc

    init = (jnp.full((tq, 1), -jnp.inf, F32), jnp.zeros((tq, 1), F32),
            jnp.zeros((tq, V_DIM), F32))
    _, l, acc = lax.fori_loop(0, i + 1, step, init)
    o_ref[...] = (acc / l).astype(BF16)


def _attention(q, k, v, batch, lp, tq):
    nq = lp // tq
    t = q.shape[1]
    return pl.pallas_call(
        partial(_attn_kernel, tq=tq),
        grid=(batch, N_HEADS, nq),
        in_specs=[pl.BlockSpec((1, tq, HEAD_PAD), lambda b, h, i: (h, b * nq + i, 0)),
                  pl.BlockSpec((1, lp, HEAD_PAD), lambda b, h, i: (h, b, 0)),
                  pl.BlockSpec((1, lp, V_DIM), lambda b, h, i: (h, b, 0))],
        out_specs=pl.BlockSpec((tq, V_DIM), lambda b, h, i: (b * nq + i, h)),
        out_shape=jax.ShapeDtypeStruct((t, MLA_WIDTH), BF16),
        compiler_params=pltpu.CompilerParams(
            dimension_semantics=("arbitrary", "arbitrary", "arbitrary"),
            vmem_limit_bytes=VMEM_LIMIT),
        name="prompt_attn",
    )(q, k, v)


def _pool_project(mixed_fn, wpool_ref, pscale_ref, wo_ref):
    out = None
    for g in range(len(POOL_WINDOWS)):
        lo = g * POOL_GROUP
        y = jnp.dot(mixed_fn(g).astype(BF16), wpool_ref[g], preferred_element_type=F32)
        y = y * pscale_ref[:, lo:lo + POOL_GROUP]
        part = jnp.dot(y.astype(BF16), wo_ref[MLA_WIDTH + lo:MLA_WIDTH + lo + POOL_GROUP, :],
                       preferred_element_type=F32)
        out = part if out is None else out + part
    return out


def _mix_out_kernel(x_ref, o_ref, pool_ref, halo_ref, wpool_ref, pscale_ref, wo_ref, gm_ref,
                    h_ref, hn_ref, ext_ref, *, tm, lp):
    i = pl.program_id(0)
    halo = halo_ref[...]
    ext_ref[0:N_META, :] = jnp.where(i == 0, jnp.zeros_like(halo), halo)
    ext_ref[N_META:, :] = pool_ref[...]
    row = i * tm + lax.broadcasted_iota(jnp.int32, (tm, 1), 0)
    pos = row
    for b in range(1, 8):
        pos = jnp.where(row >= b * lp, row - b * lp, pos)

    def mixed(g):
        w = POOL_WINDOWS[g]
        lo = g * POOL_GROUP
        cur = ext_ref[N_META:N_META + tm, lo:lo + POOL_GROUP]
        tot = cur
        for kk in range(1, w):
            tot = tot + ext_ref[N_META - kk:N_META - kk + tm, lo:lo + POOL_GROUP]
        cnt = jnp.minimum(w, pos + 1).astype(F32)
        return tot / cnt - cur

    acc = jnp.dot(o_ref[...], wo_ref[:MLA_WIDTH, :], preferred_element_type=F32)
    acc = acc + _pool_project(mixed, wpool_ref, pscale_ref, wo_ref)
    h = x_ref[...] + acc
    h_ref[...] = h
    hn_ref[...] = _rms(h, gm_ref[...]).astype(BF16)


def _mix_out(x, o, pool, wpool, pscale, wo, gm, tm, lp):
    t = x.shape[0]
    assert t <= 8 * lp and tm % N_META == 0
    row = lambda w: pl.BlockSpec((tm, w), lambda i: (i, 0))
    hb = tm // N_META
    return pl.pallas_call(
        partial(_mix_out_kernel, tm=tm, lp=lp),
        grid=(t // tm,),
        in_specs=[row(D_MODEL), row(MLA_WIDTH), row(POOL_WIDTH),
                  pl.BlockSpec((N_META, POOL_WIDTH), lambda i: (jnp.maximum(i * hb - 1, 0), 0)),
                  _const_spec((len(POOL_WINDOWS), POOL_GROUP, POOL_GROUP)),
                  _const_spec((1, POOL_WIDTH)), _const_spec((D_MODEL, D_MODEL)),
                  _const_spec((1, D_MODEL))],
        out_specs=[row(D_MODEL), row(D_MODEL)],
        out_shape=[jax.ShapeDtypeStruct((t, D_MODEL), F32),
                   jax.ShapeDtypeStruct((t, D_MODEL), BF16)],
        scratch_shapes=[pltpu.VMEM((tm + N_META, POOL_WIDTH), F32)],
        compiler_params=pltpu.CompilerParams(
            dimension_semantics=("arbitrary",), vmem_limit_bytes=VMEM_LIMIT),
        name="mix_out",
    )(x, o, pool, pool, wpool, pscale, wo, gm)


def _mix_out_sample_kernel(x_ref, o_ref, pin_ref, st_ref, wpool_ref, pscale_ref, wo_ref, gm_ref,
                           h_ref, hn_ref):
    def mixed(g):
        w = POOL_WINDOWS[g]
        lo = g * POOL_GROUP
        cur = pin_ref[:, lo:lo + POOL_GROUP]
        tot = cur
        for kk in range(1, w):
            base = (POOL_HIST - kk) * POOL_WIDTH + lo
            tot = tot + st_ref[:, base:base + POOL_GROUP]
        return tot / float(w) - cur

    acc = jnp.dot(o_ref[...].astype(BF16), wo_ref[:MLA_WIDTH, :], preferred_element_type=F32)
    acc = acc + _pool_project(mixed, wpool_ref, pscale_ref, wo_ref)
    h = x_ref[...] + acc
    h_ref[...] = h
    hn_ref[...] = _rms(h, gm_ref[...]).astype(BF16)


def _mix_out_sample(x, o, pin, state, wpool, pscale, wo, gm):
    t = x.shape[0]
    full = lambda a: pl.BlockSpec(a.shape, lambda: (0,) * a.ndim)
    args = (x, o, pin, state, wpool, pscale, wo, gm)
    return pl.pallas_call(
        _mix_out_sample_kernel,
        in_specs=[full(a) for a in args],
        out_specs=[pl.BlockSpec((t, D_MODEL), lambda: (0, 0))] * 2,
        out_shape=[jax.ShapeDtypeStruct((t, D_MODEL), F32),
                   jax.ShapeDtypeStruct((t, D_MODEL), BF16)],
        compiler_params=pltpu.CompilerParams(vmem_limit_bytes=VMEM_LIMIT),
        name="mix_out_sample",
    )(*args)


def _mlp_kernel(h_ref, hn_ref, wu_ref, wd_ref, gf_ref, y_ref, acc_ref):
    f = pl.program_id(1)
    z = jnp.dot(hn_ref[...], wu_ref[...], preferred_element_type=F32)
    a = jnp.square(jnp.maximum(z, 0.0)).astype(BF16)
    part = jnp.dot(a, wd_ref[...], preferred_element_type=F32)

    @pl.when(f == 0)
    def _():
        acc_ref[...] = h_ref[...] + part

    @pl.when(f > 0)
    def _():
        acc_ref[...] += part

    @pl.when(f == pl.num_programs(1) - 1)
    def _():
        y_ref[...] = _rms(acc_ref[...], gf_ref[...])


def _mlp(h, hn, wu, wd, gf, tm, tf):
    t = h.shape[0]
    return pl.pallas_call(
        _mlp_kernel,
        grid=(t // tm, D_FF // tf),
        in_specs=[pl.BlockSpec((tm, D_MODEL), lambda i, f: (i, 0)),
                  pl.BlockSpec((tm, D_MODEL), lambda i, f: (i, 0)),
                  pl.BlockSpec((D_MODEL, tf), lambda i, f: (0, f)),
                  pl.BlockSpec((tf, D_MODEL), lambda i, f: (f, 0)),
                  pl.BlockSpec((1, D_MODEL), lambda i, f: (0, 0))],
        out_specs=pl.BlockSpec((tm, D_MODEL), lambda i, f: (i, 0)),
        out_shape=jax.ShapeDtypeStruct((t, D_MODEL), F32),
        scratch_shapes=[pltpu.VMEM((tm, D_MODEL), F32)],
        compiler_params=pltpu.CompilerParams(
            dimension_semantics=("arbitrary", "arbitrary"), vmem_limit_bytes=VMEM_LIMIT),
        name="mlp",
    )(h, hn, wu, wd, gf)


def _absorb_kernel(q_ref, wk_ref, o_ref):
    q = q_ref[0]
    qlat = lax.dot_general(q[:, :LANES], wk_ref[...], (((1,), (1,)), ((), ())),
                           preferred_element_type=F32)
    o_ref[:, :KV_RANK] = qlat
    o_ref[:, KV_RANK:] = q[:, LANES:].astype(F32)


def _absorb(q, wk):
    db = q.shape[1]
    return pl.pallas_call(
        _absorb_kernel,
        grid=(N_HEADS,),
        in_specs=[pl.BlockSpec((1, db, HEAD_PAD), lambda h: (h, 0, 0)),
                  pl.BlockSpec((KV_RANK, LANES), lambda h: (0, h))],
        out_specs=pl.BlockSpec((db, QCAT), lambda h: (0, h)),
        out_shape=jax.ShapeDtypeStruct((db, N_HEADS * QCAT), F32),
        compiler_params=pltpu.CompilerParams(dimension_semantics=("arbitrary",)),
        name="absorb_q",
    )(q, wk)


def _decode_kernel(pt_ref, q_ref, cown_ref, kown_ref, *refs, n_pages):
    lat_refs = refs[:n_pages]
    kr_refs = refs[n_pages:2 * n_pages]
    o_ref = refs[2 * n_pages]
    m_ref, l_ref, acc_ref = refs[2 * n_pages + 1:]
    g = pl.program_id(1)
    q = q_ref[0]
    qlat = q[:, :KV_RANK]
    qpe = q[:, KV_RANK:KV_RANK + QK_ROPE]

    @pl.when(g == 0)
    def _():
        c_own = cown_ref[0]
        k_own = kown_ref[0]
        s_own = (jnp.sum(qlat * c_own, axis=-1, keepdims=True)
                 + jnp.sum(qpe * k_own[:, :QK_ROPE], axis=-1, keepdims=True))
        m_ref[...] = s_own
        l_ref[...] = jnp.ones_like(s_own)
        acc_ref[...] = jnp.broadcast_to(c_own, acc_ref.shape)

    nt = (((1,), (1,)), ((), ()))
    scores = [lax.dot_general(qlat, lat_refs[k][0], nt, preferred_element_type=F32)
              + lax.dot_general(qpe, kr_refs[k][0], nt, preferred_element_type=F32)
              for k in range(n_pages)]
    s = jnp.concatenate(scores, axis=-1)
    m = m_ref[...]
    m_new = jnp.maximum(m, jnp.max(s, axis=-1, keepdims=True))
    corr = jnp.exp(m - m_new)
    p = jnp.exp(s - m_new)
    l_ref[...] = l_ref[...] * corr + jnp.sum(p, axis=-1, keepdims=True)
    acc = acc_ref[...] * corr
    for k in range(n_pages):
        acc = acc + jnp.dot(p[:, k * PAGE_SIZE:(k + 1) * PAGE_SIZE], lat_refs[k][0],
                            preferred_element_type=F32)
    acc_ref[...] = acc
    m_ref[...] = m_new

    @pl.when(g == pl.num_programs(1) - 1)
    def _():
        o_ref[0] = acc_ref[...] / l_ref[...]


def _decode(page_table, qcat, c_own, k_own, cache_lat, cache_kr):
    db, n_pages = page_table.shape
    gp = PAGES_PER_STEP
    lat_spec = lambda k: pl.BlockSpec(
        (1, PAGE_SIZE, KV_RANK), lambda b, g, pt: (pt[b, g * gp + k], 0, 0))
    kr_spec = lambda k: pl.BlockSpec(
        (1, PAGE_SIZE, QK_ROPE), lambda b, g, pt: (pt[b, g * gp + k], 0, 0))
    grid_spec = pltpu.PrefetchScalarGridSpec(
        num_scalar_prefetch=1,
        grid=(db, n_pages // gp),
        in_specs=[pl.BlockSpec((1, N_HEADS, QCAT), lambda b, g, pt: (b, 0, 0)),
                  pl.BlockSpec((1, 1, KV_RANK), lambda b, g, pt: (b, 0, 0)),
                  pl.BlockSpec((1, 1, LANES), lambda b, g, pt: (b, 0, 0))]
                 + [lat_spec(k) for k in range(gp)] + [kr_spec(k) for k in range(gp)],
        out_specs=pl.BlockSpec((1, N_HEADS, KV_RANK), lambda b, g, pt: (b, 0, 0)),
        scratch_shapes=[pltpu.VMEM((N_HEADS, 1), F32), pltpu.VMEM((N_HEADS, 1), F32),
                        pltpu.VMEM((N_HEADS, KV_RANK), F32)],
    )
    return pl.pallas_call(
        partial(_decode_kernel, n_pages=gp),
        grid_spec=grid_spec,
        out_shape=jax.ShapeDtypeStruct((db, N_HEADS, KV_RANK), F32),
        compiler_params=pltpu.CompilerParams(
            dimension_semantics=("arbitrary", "arbitrary"), vmem_limit_bytes=VMEM_LIMIT),
        name="paged_decode",
    )(page_table, qcat, c_own, k_own, *([cache_lat] * gp), *([cache_kr] * gp))


def _unabsorb_kernel(lat_ref, wv_ref, o_ref):
    o_ref[...] = jnp.dot(lat_ref[...].astype(BF16), wv_ref[...], preferred_element_type=F32)


def _unabsorb(lat, wv):
    db = lat.shape[0]
    return pl.pallas_call(
        _unabsorb_kernel,
        grid=(N_HEADS,),
        in_specs=[pl.BlockSpec((db, KV_RANK), lambda h: (0, h)),
                  pl.BlockSpec((KV_RANK, V_DIM), lambda h: (0, h))],
        out_specs=pl.BlockSpec((db, V_DIM), lambda h: (0, h)),
        out_shape=jax.ShapeDtypeStruct((db, MLA_WIDTH), F32),
        compiler_params=pltpu.CompilerParams(dimension_semantics=("arbitrary",)),
        name="unabsorb_v",
    )(lat, wv)


def _rot_cols(w):
    half = w.shape[-1] // 2
    return jnp.concatenate([-w[..., half:], w[..., :half]], axis=-1)


def _rope_tables(pos):
    inv = ROPE_THETA ** (-jnp.arange(0, QK_ROPE, 2, dtype=F32) / QK_ROPE)
    ang = pos.astype(F32)[:, None] * inv[None, :]
    ang = jnp.concatenate([ang, ang], axis=-1)
    z = jnp.zeros_like(ang)
    return (jnp.concatenate([jnp.cos(ang), z], axis=-1),
            jnp.concatenate([jnp.sin(ang), z], axis=-1))


def kernel(x_prompt, x_sample, cache_kv_latent, cache_k_rope, state_pool, page_table, meta_tokens,
           g_attn, w_in, g_q, w_q_b, g_kv, w_kv_b, w_pool, pool_scale, w_o, g_mlp, w_up, w_down,
           g_final):
    assert g_attn.shape[0] == 1, "single-layer trunk"
    batch, seq, _ = x_prompt.shape
    db, s1, _ = x_sample.shape
    assert s1 == 1
    L = seq + N_META
    lp = -(-L // LANES) * LANES
    assert lp - L >= POOL_HIST

    kr_lo = Q_RANK + KV_RANK
    win = w_in[0]
    win_r = jnp.concatenate([win[:, :kr_lo + QK_ROPE], _rot_cols(win[:, kr_lo:kr_lo + QK_ROPE]),
                             win[:, kr_lo + QK_ROPE:]], axis=1).astype(BF16)
    wq = w_q_b[0].reshape(Q_RANK, N_HEADS, QK_NOPE + QK_ROPE)
    wq_r = jnp.concatenate([wq, _rot_cols(wq[..., QK_NOPE:])], axis=-1)
    wq_r = wq_r.reshape(Q_RANK, N_HEADS * HEAD_PAD).astype(BF16)
    wk = w_kv_b[0][..., :QK_NOPE].reshape(KV_RANK, MLA_WIDTH).astype(BF16)
    wv = w_kv_b[0][..., QK_NOPE:].reshape(KV_RANK, MLA_WIDTH).astype(BF16)
    wpool = w_pool[0].astype(BF16)
    wo = w_o[0].astype(BF16)
    wu = w_up[0].astype(BF16)
    wd = w_down[0].astype(BF16)
    ga, gq, gkv, gm = g_attn, g_q, g_kv, g_mlp
    gf = g_final[None, :]
    pscale = pool_scale

    meta = jnp.broadcast_to(meta_tokens[None], (batch, N_META, D_MODEL))
    hp = jnp.concatenate([meta, x_prompt, jnp.zeros((batch, lp - L, D_MODEL), F32)], axis=1)
    hp = hp.reshape(batch * lp, D_MODEL)
    cos_p, sin_p = _rope_tables(jnp.arange(lp, dtype=jnp.int32))
    cos_p = jnp.tile(cos_p, (batch, 1))
    sin_p = jnp.tile(sin_p, (batch, 1))
    lat_p, kpe_p, pool_p, q_p, k_p, v_p = _mixer_in(hp, cos_p, sin_p, ga, win_r, gq, wq_r, gkv,
                                                    wk, wv, tm=256)
    o_p = _attention(q_p, k_p, v_p, batch, lp, tq=LANES)
    h_p, hn_p = _mix_out(hp, o_p, pool_p, wpool, pscale, wo, gm, tm=256, lp=lp)
    y_p = _mlp(h_p, hn_p, wu, wd, gf, tm=512, tf=512)

    y_prompt = y_p.reshape(batch, lp, D_MODEL)[:, N_META:L]
    new_lat_prompt = lat_p.reshape(batch, lp, KV_RANK)[None, :, :L]
    new_krope_prompt = kpe_p.reshape(batch, lp, LANES)[None, :, :L, :QK_ROPE]
    new_pool_prompt = pool_p.reshape(batch, lp, POOL_WIDTH)[None, :, L - POOL_HIST:L]

    xs = x_sample.reshape(db, D_MODEL)
    cos_s, sin_s = _rope_tables(jnp.full((db,), PAST_LEN, jnp.int32))
    lat_s, kpe_s, pool_s, q_s, _, _ = _mixer_in(xs, cos_s, sin_s, ga, win_r, gq, wq_r, gkv,
                                                wk, wv, tm=db)
    qcat = _absorb(q_s, wk).reshape(db, N_HEADS, QCAT)
    n_phys = cache_kv_latent.shape[1]
    dec = _decode(page_table, qcat, lat_s.reshape(db, 1, KV_RANK), kpe_s.reshape(db, 1, LANES),
                  cache_kv_latent.reshape(n_phys, PAGE_SIZE, KV_RANK),
                  cache_k_rope.reshape(n_phys, PAGE_SIZE, QK_ROPE))
    a_s = _unabsorb(dec.reshape(db, N_HEADS * KV_RANK), wv)
    state = state_pool[0]
    h_s, hn_s = _mix_out_sample(xs, a_s, pool_s, state.reshape(db, POOL_HIST * POOL_WIDTH),
                                wpool, pscale, wo, gm)
    y_s = _mlp(h_s, hn_s, wu, wd, gf, tm=db, tf=512)

    y_sample = y_s.reshape(db, 1, D_MODEL)
    new_lat_sample = lat_s.reshape(1, db, 1, KV_RANK)
    new_krope_sample = kpe_s[:, :QK_ROPE].reshape(1, db, 1, QK_ROPE)
    new_pool_sample = jnp.concatenate([state[:, 1:], pool_s[:, None, :]], axis=1)[None]
    return (y_prompt, y_sample, new_lat_prompt, new_krope_prompt, new_pool_prompt,
            new_lat_sample, new_krope_sample, new_pool_sample)
```

```python
from functools import partial

import jax
import jax.numpy as jnp
from jax import lax
from jax.experimental import pallas as pl
from jax.experimental.pallas import tpu as pltpu

D_MODEL = 2048
N_META = 16
N_HEADS = 12
QK_NOPE = 128
QK_ROPE = 64
V_DIM = 128
Q_RANK = 512
KV_RANK = 512
MLA_WIDTH = N_HEADS * V_DIM
POOL_WIDTH = D_MODEL - MLA_WIDTH
POOL_WINDOWS = (2, 4, 8, 16)
POOL_GROUP = 128
POOL_HIST = 15
D_FF = 4 * D_MODEL
ROPE_THETA = 10000.0
NORM_EPS = 1e-6
PAST_LEN = 16384
PAGE_SIZE = 128

LANES = 128
HEAD_PAD = 256
HEADS_PAD = 16
QCAT = 640
IN_COLS_R = Q_RANK + KV_RANK + 2 * QK_ROPE + POOL_WIDTH
SCALE = (QK_NOPE + QK_ROPE) ** -0.5
VMEM_LIMIT = 56 * 1024 * 1024
DECODE_GROUP_PAGES = 16
DECODE_SLOTS = 4

BF16 = jnp.bfloat16
F32 = jnp.float32


def _rms(x, g):
    return x * lax.rsqrt(jnp.mean(x * x, axis=-1, keepdims=True) + NORM_EPS) * g


def _rope128(x, c, s):
    return x * c + pltpu.roll(x, QK_ROPE, axis=1) * s


def _const_spec(shape):
    n = len(shape)
    return pl.BlockSpec(shape, lambda *_: (0,) * n, pipeline_mode=pl.Buffered(1))


def _mixer_in_kernel(x_ref, ga_ref, win_ref, gq_ref, wq_ref, gkv_ref, wk_ref, wv_ref,
                     c_ref, s_ref, lat_ref, kpe_ref, pool_ref, q_ref, k_ref, v_ref):
    h = _rms(x_ref[...], ga_ref[...]).astype(BF16)
    proj = jnp.dot(h, win_ref[...], preferred_element_type=F32)
    cos = c_ref[...]
    sin = s_ref[...]
    ckv = _rms(proj[:, Q_RANK:Q_RANK + KV_RANK], gkv_ref[...])
    lat_ref[...] = ckv
    kpe = _rope128(proj[:, 2 * Q_RANK:2 * Q_RANK + LANES], cos, sin)
    kpe_ref[...] = kpe
    pool_ref[...] = proj[:, 2 * Q_RANK + LANES:]
    kpe_b = kpe.astype(BF16)

    qn = _rms(proj[:, :Q_RANK], gq_ref[...]).astype(BF16)
    q = jnp.dot(qn, wq_ref[...], preferred_element_type=F32)
    for hh in range(N_HEADS):
        base = hh * HEAD_PAD
        q_ref[hh, :, :LANES] = (q[:, base:base + LANES] * SCALE).astype(BF16)
        pe = _rope128(q[:, base + LANES:base + HEAD_PAD], cos, sin) * SCALE
        q_ref[hh, :, LANES:] = pe.astype(BF16)

    cb = ckv.astype(BF16)
    kn = jnp.dot(cb, wk_ref[...], preferred_element_type=F32)
    vv = jnp.dot(cb, wv_ref[...], preferred_element_type=F32)
    for hh in range(N_HEADS):
        k_ref[hh, :, :LANES] = kn[:, hh * LANES:(hh + 1) * LANES].astype(BF16)
        k_ref[hh, :, LANES:] = kpe_b
        v_ref[hh] = vv[:, hh * LANES:(hh + 1) * LANES].astype(BF16)


def _mixer_in(x, cos, sin, ga, win, gq, wq, gkv, wk, wv, tm):
    t = x.shape[0]
    row = lambda w: pl.BlockSpec((tm, w), lambda i: (i, 0))
    head = lambda w: pl.BlockSpec((N_HEADS, tm, w), lambda i: (0, i, 0))
    return pl.pallas_call(
        _mixer_in_kernel,
        grid=(t // tm,),
        in_specs=[row(D_MODEL), _const_spec((1, D_MODEL)), _const_spec((D_MODEL, IN_COLS_R)),
                  _const_spec((1, Q_RANK)), _const_spec((Q_RANK, N_HEADS * HEAD_PAD)),
                  _const_spec((1, KV_RANK)), _const_spec((KV_RANK, MLA_WIDTH)),
                  _const_spec((KV_RANK, MLA_WIDTH)), row(LANES), row(LANES)],
        out_specs=[row(KV_RANK), row(LANES), row(POOL_WIDTH),
                   head(HEAD_PAD), head(HEAD_PAD), head(V_DIM)],
        out_shape=[jax.ShapeDtypeStruct((t, KV_RANK), F32),
                   jax.ShapeDtypeStruct((t, LANES), F32),
                   jax.ShapeDtypeStruct((t, POOL_WIDTH), F32),
                   jax.ShapeDtypeStruct((N_HEADS, t, HEAD_PAD), BF16),
                   jax.ShapeDtypeStruct((N_HEADS, t, HEAD_PAD), BF16),
                   jax.ShapeDtypeStruct((N_HEADS, t, V_DIM), BF16)],
        compiler_params=pltpu.CompilerParams(
            dimension_semantics=("arbitrary",), vmem_limit_bytes=VMEM_LIMIT),
        name="mixer_in",
    )(x, ga, win, gq, wq, gkv, wk, wv, cos, sin)


_NT = (((1,), (1,)), ((), ()))


def _attn_kernel(q_ref, k_ref, v_ref, o_ref, *, tq, lp):
    for r0 in range(0, lp, tq):
        nr = min(tq, lp - r0)
        q = q_ref[0, r0:r0 + nr, :]
        sd = lax.dot_general(q, k_ref[0, r0:r0 + nr, :], _NT, preferred_element_type=F32)
        row = lax.broadcasted_iota(jnp.int32, (nr, nr), 0)
        col = lax.broadcasted_iota(jnp.int32, (nr, nr), 1)
        sd = jnp.where(col <= row, sd, -jnp.inf)
        m = jnp.max(sd, axis=-1, keepdims=True)
        if r0:
            sl = lax.dot_general(q, k_ref[0, :r0, :], _NT, preferred_element_type=F32)
            m = jnp.maximum(m, jnp.max(sl, axis=-1, keepdims=True))
            p_left = jnp.exp(sl - m)
        p_diag = jnp.exp(sd - m)
        l = jnp.sum(p_diag, axis=-1, keepdims=True)
        o = jnp.dot(p_diag.astype(BF16), v_ref[0, r0:r0 + nr, :], preferred_element_type=F32)
        if r0:
            l = l + jnp.sum(p_left, axis=-1, keepdims=True)
            o = o + jnp.dot(p_left.astype(BF16), v_ref[0, :r0, :], preferred_element_type=F32)
        o_ref[r0:r0 + nr, :] = (o / l).astype(BF16)


def _attention(q, k, v, batch, lp, tq):
    t = q.shape[1]
    seq = lambda w: pl.BlockSpec((1, lp, w), lambda b, h: (h, b, 0))
    return pl.pallas_call(
        partial(_attn_kernel, tq=tq, lp=lp),
        grid=(batch, N_HEADS),
        in_specs=[seq(HEAD_PAD), seq(HEAD_PAD), seq(V_DIM)],
        out_specs=pl.BlockSpec((lp, V_DIM), lambda b, h: (b, h)),
        out_shape=jax.ShapeDtypeStruct((t, MLA_WIDTH), BF16),
        compiler_params=pltpu.CompilerParams(
            dimension_semantics=("arbitrary", "arbitrary"), vmem_limit_bytes=VMEM_LIMIT),
        name="prompt_attn",
    )(q, k, v)


def _pool_project(mixed_fn, wpool_ref, pscale_ref, wo_ref):
    out = None
    for g in range(len(POOL_WINDOWS)):
        lo = g * POOL_GROUP
        y = jnp.dot(mixed_fn(g).astype(BF16), wpool_ref[g], preferred_element_type=F32)
        y = y * pscale_ref[:, lo:lo + POOL_GROUP]
        part = jnp.dot(y.astype(BF16), wo_ref[MLA_WIDTH + lo:MLA_WIDTH + lo + POOL_GROUP, :],
                       preferred_element_type=F32)
        out = part if out is None else out + part
    return out


def _mix_out_kernel(x_ref, o_ref, pool_ref, halo_ref, wpool_ref, pscale_ref, wo_ref, gm_ref,
                    h_ref, hn_ref, ext_ref, *, tm, lp):
    i = pl.program_id(0)
    halo = halo_ref[...]
    ext_ref[0:N_META, :] = jnp.where(i == 0, jnp.zeros_like(halo), halo)
    ext_ref[N_META:, :] = pool_ref[...]
    row = i * tm + lax.broadcasted_iota(jnp.int32, (tm, 1), 0)
    pos = row
    for b in range(1, 8):
        pos = jnp.where(row >= b * lp, row - b * lp, pos)

    def mixed(g):
        w = POOL_WINDOWS[g]
        lo = g * POOL_GROUP
        cur = ext_ref[N_META:N_META + tm, lo:lo + POOL_GROUP]
        tot = cur
        for kk in range(1, w):
            tot = tot + ext_ref[N_META - kk:N_META - kk + tm, lo:lo + POOL_GROUP]
        cnt = jnp.minimum(w, pos + 1).astype(F32)
        return tot / cnt - cur

    acc = jnp.dot(o_ref[...], wo_ref[:MLA_WIDTH, :], preferred_element_type=F32)
    acc = acc + _pool_project(mixed, wpool_ref, pscale_ref, wo_ref)
    h = x_ref[...] + acc
    h_ref[...] = h
    hn_ref[...] = _rms(h, gm_ref[...]).astype(BF16)


def _mix_out(x, o, pool, wpool, pscale, wo, gm, tm, lp):
    t = x.shape[0]
    assert t <= 8 * lp and tm % N_META == 0
    row = lambda w: pl.BlockSpec((tm, w), lambda i: (i, 0))
    hb = tm // N_META
    return pl.pallas_call(
        partial(_mix_out_kernel, tm=tm, lp=lp),
        grid=(t // tm,),
        in_specs=[row(D_MODEL), row(MLA_WIDTH), row(POOL_WIDTH),
                  pl.BlockSpec((N_META, POOL_WIDTH), lambda i: (jnp.maximum(i * hb - 1, 0), 0)),
                  _const_spec((len(POOL_WINDOWS), POOL_GROUP, POOL_GROUP)),
                  _const_spec((1, POOL_WIDTH)), _const_spec((D_MODEL, D_MODEL)),
                  _const_spec((1, D_MODEL))],
        out_specs=[row(D_MODEL), row(D_MODEL)],
        out_shape=[jax.ShapeDtypeStruct((t, D_MODEL), F32),
                   jax.ShapeDtypeStruct((t, D_MODEL), BF16)],
        scratch_shapes=[pltpu.VMEM((tm + N_META, POOL_WIDTH), F32)],
        compiler_params=pltpu.CompilerParams(
            dimension_semantics=("arbitrary",), vmem_limit_bytes=VMEM_LIMIT),
        name="mix_out",
    )(x, o, pool, pool, wpool, pscale, wo, gm)


def _mix_out_sample_kernel(x_ref, o_ref, pin_ref, st_ref, wpool_ref, pscale_ref, wo_ref, gm_ref,
                           h_ref, hn_ref):
    def mixed(g):
        w = POOL_WINDOWS[g]
        lo = g * POOL_GROUP
        cur = pin_ref[:, lo:lo + POOL_GROUP]
        tot = cur
        for kk in range(1, w):
            base = (POOL_HIST - kk) * POOL_WIDTH + lo
            tot = tot + st_ref[:, base:base + POOL_GROUP]
        return tot / float(w) - cur

    acc = jnp.dot(o_ref[...].astype(BF16), wo_ref[:MLA_WIDTH, :], preferred_element_type=F32)
    acc = acc + _pool_project(mixed, wpool_ref, pscale_ref, wo_ref)
    h = x_ref[...] + acc
    h_ref[...] = h
    hn_ref[...] = _rms(h, gm_ref[...]).astype(BF16)


def _mix_out_sample(x, o, pin, state, wpool, pscale, wo, gm):
    t = x.shape[0]
    full = lambda a: pl.BlockSpec(a.shape, lambda: (0,) * a.ndim)
    args = (x, o, pin, state, wpool, pscale, wo, gm)
    return pl.pallas_call(
        _mix_out_sample_kernel,
        in_specs=[full(a) for a in args],
        out_specs=[pl.BlockSpec((t, D_MODEL), lambda: (0, 0))] * 2,
        out_shape=[jax.ShapeDtypeStruct((t, D_MODEL), F32),
                   jax.ShapeDtypeStruct((t, D_MODEL), BF16)],
        compiler_params=pltpu.CompilerParams(vmem_limit_bytes=VMEM_LIMIT),
        name="mix_out_sample",
    )(*args)


def _mlp_kernel(h_ref, hn_ref, wu_ref, wd_ref, gf_ref, y_ref, acc_ref):
    f = pl.program_id(1)
    z = jnp.dot(hn_ref[...], wu_ref[...], preferred_element_type=F32)
    a = jnp.square(jnp.maximum(z, 0.0)).astype(BF16)
    part = jnp.dot(a, wd_ref[...], preferred_element_type=F32)

    @pl.when(f == 0)
    def _():
        acc_ref[...] = h_ref[...] + part

    @pl.when(f > 0)
    def _():
        acc_ref[...] += part

    @pl.when(f == pl.num_programs(1) - 1)
    def _():
        y_ref[...] = _rms(acc_ref[...], gf_ref[...])


def _mlp(h, hn, wu, wd, gf, tm, tf):
    t = h.shape[0]
    return pl.pallas_call(
        _mlp_kernel,
        grid=(t // tm, D_FF // tf),
        in_specs=[pl.BlockSpec((tm, D_MODEL), lambda i, f: (i, 0)),
                  pl.BlockSpec((tm, D_MODEL), lambda i, f: (i, 0)),
                  pl.BlockSpec((D_MODEL, tf), lambda i, f: (0, f)),
                  pl.BlockSpec((tf, D_MODEL), lambda i, f: (f, 0)),
                  pl.BlockSpec((1, D_MODEL), lambda i, f: (0, 0))],
        out_specs=pl.BlockSpec((tm, D_MODEL), lambda i, f: (i, 0)),
        out_shape=jax.ShapeDtypeStruct((t, D_MODEL), F32),
        scratch_shapes=[pltpu.VMEM((tm, D_MODEL), F32)],
        compiler_params=pltpu.CompilerParams(
            dimension_semantics=("arbitrary", "arbitrary"), vmem_limit_bytes=VMEM_LIMIT),
        name="mlp",
    )(h, hn, wu, wd, gf)


def _absorb_kernel(q_ref, wk_ref, o_ref):
    q = q_ref[0]
    qlat = lax.dot_general(q[:, :LANES], wk_ref[...], _NT,
                           preferred_element_type=F32)
    o_ref[:, :KV_RANK] = qlat
    o_ref[:, KV_RANK:] = q[:, LANES:].astype(F32)


def _absorb(q, wk):
    db = q.shape[1]
    return pl.pallas_call(
        _absorb_kernel,
        grid=(N_HEADS,),
        in_specs=[pl.BlockSpec((1, db, HEAD_PAD), lambda h: (h, 0, 0)),
                  pl.BlockSpec((KV_RANK, LANES), lambda h: (0, h))],
        out_specs=pl.BlockSpec((db, QCAT), lambda h: (0, h)),
        out_shape=jax.ShapeDtypeStruct((db, N_HEADS * QCAT), F32),
        compiler_params=pltpu.CompilerParams(dimension_semantics=("arbitrary",)),
        name="absorb_q",
    )(q, wk)


def _decode_kernel(pt_ref, q_ref, cown_ref, kown_ref, lat_hbm, krt_hbm, o_ref,
                   latbuf, krtbuf, sem, *, gp, ng, nslot):
    b = pl.program_id(0)
    nb = pl.num_programs(0)

    def group_copies(bb, gi):
        slot = gi % nslot
        cps = []
        for k in range(gp):
            page = pt_ref[bb, gi * gp + k]
            cps.append(pltpu.make_async_copy(lat_hbm.at[page], latbuf.at[slot, k], sem.at[slot]))
            cps.append(pltpu.make_async_copy(krt_hbm.at[page], krtbuf.at[slot, k], sem.at[slot]))
        return cps

    def start(bb, gi):
        for cp in group_copies(bb, gi):
            cp.start()

    def wait(bb, gi):
        for cp in group_copies(bb, gi):
            cp.wait()

    @pl.when(b == 0)
    def _():
        for gi in range(nslot):
            start(0, gi)

    q = q_ref[0]
    qlat = q[:, :KV_RANK]
    qpe = q[:, KV_RANK:KV_RANK + QK_ROPE]

    def scores(gi):
        slot = gi % nslot
        parts = [lax.dot_general(qlat, latbuf[slot, k], _NT, preferred_element_type=F32)
                 + jnp.dot(qpe, krtbuf[slot, k], preferred_element_type=F32)
                 for k in range(gp)]
        return jnp.concatenate(parts, axis=-1)

    def fold(gi, s, m, l, acc):
        slot = gi % nslot
        m_new = jnp.maximum(m, jnp.max(s, axis=-1, keepdims=True))
        corr = jnp.exp(m - m_new)
        p = jnp.exp(s - m_new)
        l = l * corr + jnp.sum(p, axis=-1, keepdims=True)
        acc = acc * corr
        for k in range(gp):
            acc = acc + jnp.dot(p[:, k * PAGE_SIZE:(k + 1) * PAGE_SIZE], latbuf[slot, k],
                                preferred_element_type=F32)
        return m_new, l, acc

    c_own = cown_ref[0]
    k_own = kown_ref[0]
    m = (jnp.sum(qlat * c_own, axis=-1, keepdims=True)
         + jnp.sum(qpe * k_own[:, :QK_ROPE], axis=-1, keepdims=True))
    l = jnp.ones_like(m)
    acc = jnp.broadcast_to(c_own, (N_HEADS, KV_RANK))

    wait(b, 0)
    s_cur = scores(0)
    for gi in range(ng):
        if gi + 1 < ng:
            wait(b, gi + 1)
            s_next = scores(gi + 1)
        m, l, acc = fold(gi, s_cur, m, l, acc)
        nxt = gi + nslot
        if nxt < ng:
            start(b, nxt)
        else:
            @pl.when(b + 1 < nb)
            def _(nxt=nxt):
                start(b + 1, nxt - ng)
        if gi + 1 < ng:
            s_cur = s_next
    o_ref[0] = acc / l


def _decode(page_table, qcat, c_own, k_own, cache_lat, cache_krt):
    db, n_pages = page_table.shape
    gp, nslot = DECODE_GROUP_PAGES, DECODE_SLOTS
    ng = n_pages // gp
    assert n_pages % gp == 0 and ng % nslot == 0
    per_seq = lambda shape: pl.BlockSpec(shape, lambda b, pt: (b, 0, 0))
    grid_spec = pltpu.PrefetchScalarGridSpec(
        num_scalar_prefetch=1,
        grid=(db,),
        in_specs=[per_seq((1, N_HEADS, QCAT)), per_seq((1, 1, KV_RANK)), per_seq((1, 1, LANES)),
                  pl.BlockSpec(memory_space=pl.ANY), pl.BlockSpec(memory_space=pl.ANY)],
        out_specs=per_seq((1, N_HEADS, KV_RANK)),
        scratch_shapes=[pltpu.VMEM((nslot, gp, PAGE_SIZE, KV_RANK), F32),
                        pltpu.VMEM((nslot, gp, QK_ROPE, PAGE_SIZE), F32),
                        pltpu.SemaphoreType.DMA((nslot,))],
    )
    return pl.pallas_call(
        partial(_decode_kernel, gp=gp, ng=ng, nslot=nslot),
        grid_spec=grid_spec,
        out_shape=jax.ShapeDtypeStruct((db, N_HEADS, KV_RANK), F32),
        compiler_params=pltpu.CompilerParams(
            dimension_semantics=("arbitrary",), vmem_limit_bytes=VMEM_LIMIT),
        name="paged_decode",
    )(page_table, qcat, c_own, k_own, cache_lat, cache_krt)


def _unabsorb_kernel(lat_ref, wv_ref, o_ref):
    o_ref[...] = jnp.dot(lat_ref[...].astype(BF16), wv_ref[...], preferred_element_type=F32)


def _unabsorb(lat, wv):
    db = lat.shape[0]
    return pl.pallas_call(
        _unabsorb_kernel,
        grid=(N_HEADS,),
        in_specs=[pl.BlockSpec((db, KV_RANK), lambda h: (0, h)),
                  pl.BlockSpec((KV_RANK, V_DIM), lambda h: (0, h))],
        out_specs=pl.BlockSpec((db, V_DIM), lambda h: (0, h)),
        out_shape=jax.ShapeDtypeStruct((db, MLA_WIDTH), F32),
        compiler_params=pltpu.CompilerParams(dimension_semantics=("arbitrary",)),
        name="unabsorb_v",
    )(lat, wv)


def _rot_cols(w):
    half = w.shape[-1] // 2
    return jnp.concatenate([-w[..., half:], w[..., :half]], axis=-1)


def _rope_tables(pos):
    inv = ROPE_THETA ** (-jnp.arange(0, QK_ROPE, 2, dtype=F32) / QK_ROPE)
    ang = pos.astype(F32)[:, None] * inv[None, :]
    ang = jnp.concatenate([ang, ang], axis=-1)
    z = jnp.zeros_like(ang)
    return (jnp.concatenate([jnp.cos(ang), z], axis=-1),
            jnp.concatenate([jnp.sin(ang), z], axis=-1))


def kernel(x_prompt, x_sample, cache_kv_latent, cache_k_rope, state_pool, page_table, meta_tokens,
           g_attn, w_in, g_q, w_q_b, g_kv, w_kv_b, w_pool, pool_scale, w_o, g_mlp, w_up, w_down,
           g_final):
    assert g_attn.shape[0] == 1, "single-layer trunk"
    batch, seq, _ = x_prompt.shape
    db, s1, _ = x_sample.shape
    assert s1 == 1
    L = seq + N_META
    lp = -(-L // LANES) * LANES
    assert lp - L >= POOL_HIST

    kr_lo = Q_RANK + KV_RANK
    win = w_in[0]
    win_r = jnp.concatenate([win[:, :kr_lo + QK_ROPE], _rot_cols(win[:, kr_lo:kr_lo + QK_ROPE]),
                             win[:, kr_lo + QK_ROPE:]], axis=1).astype(BF16)
    wq = w_q_b[0].reshape(Q_RANK, N_HEADS, QK_NOPE + QK_ROPE)
    wq_r = jnp.concatenate([wq, _rot_cols(wq[..., QK_NOPE:])], axis=-1)
    wq_r = wq_r.reshape(Q_RANK, N_HEADS * HEAD_PAD).astype(BF16)
    wk = w_kv_b[0][..., :QK_NOPE].reshape(KV_RANK, MLA_WIDTH).astype(BF16)
    wv = w_kv_b[0][..., QK_NOPE:].reshape(KV_RANK, MLA_WIDTH).astype(BF16)
    wpool = w_pool[0].astype(BF16)
    wo = w_o[0].astype(BF16)
    wu = w_up[0].astype(BF16)
    wd = w_down[0].astype(BF16)
    ga, gq, gkv, gm = g_attn, g_q, g_kv, g_mlp
    gf = g_final[None, :]
    pscale = pool_scale

    meta = jnp.broadcast_to(meta_tokens[None], (batch, N_META, D_MODEL))
    hp = jnp.concatenate([meta, x_prompt, jnp.zeros((batch, lp - L, D_MODEL), F32)], axis=1)
    hp = hp.reshape(batch * lp, D_MODEL)
    cos_p, sin_p = _rope_tables(jnp.arange(lp, dtype=jnp.int32))
    cos_p = jnp.tile(cos_p, (batch, 1))
    sin_p = jnp.tile(sin_p, (batch, 1))
    lat_p, kpe_p, pool_p, q_p, k_p, v_p = _mixer_in(hp, cos_p, sin_p, ga, win_r, gq, wq_r, gkv,
                                                    wk, wv, tm=256)
    o_p = _attention(q_p, k_p, v_p, batch, lp, tq=256)
    h_p, hn_p = _mix_out(hp, o_p, pool_p, wpool, pscale, wo, gm, tm=256, lp=lp)
    y_p = _mlp(h_p, hn_p, wu, wd, gf, tm=512, tf=512)

    y_prompt = y_p.reshape(batch, lp, D_MODEL)[:, N_META:L]
    new_lat_prompt = lat_p.reshape(batch, lp, KV_RANK)[None, :, :L]
    new_krope_prompt = kpe_p.reshape(batch, lp, LANES)[None, :, :L, :QK_ROPE]
    new_pool_prompt = pool_p.reshape(batch, lp, POOL_WIDTH)[None, :, L - POOL_HIST:L]

    xs = x_sample.reshape(db, D_MODEL)
    cos_s, sin_s = _rope_tables(jnp.full((db,), PAST_LEN, jnp.int32))
    lat_s, kpe_s, pool_s, q_s, _, _ = _mixer_in(xs, cos_s, sin_s, ga, win_r, gq, wq_r, gkv,
                                                wk, wv, tm=db)
    qcat = _absorb(q_s, wk).reshape(db, N_HEADS, QCAT)
    n_phys = cache_kv_latent.shape[1]
    dec = _decode(page_table, qcat, lat_s.reshape(db, 1, KV_RANK), kpe_s.reshape(db, 1, LANES),
                  cache_kv_latent.reshape(n_phys, PAGE_SIZE, KV_RANK),
                  jnp.swapaxes(cache_k_rope[0], 1, 2))
    a_s = _unabsorb(dec.reshape(db, N_HEADS * KV_RANK), wv)
    state = state_pool[0]
    h_s, hn_s = _mix_out_sample(xs, a_s, pool_s, state.reshape(db, POOL_HIST * POOL_WIDTH),
                                wpool, pscale, wo, gm)
    y_s = _mlp(h_s, hn_s, wu, wd, gf, tm=db, tf=512)

    y_sample = y_s.reshape(db, 1, D_MODEL)
    new_lat_sample = lat_s.reshape(1, db, 1, KV_RANK)
    new_krope_sample = kpe_s[:, :QK_ROPE].reshape(1, db, 1, QK_ROPE)
    new_pool_sample = jnp.concatenate([state[:, 1:], pool_s[:, None, :]], axis=1)[None]
    return (y_prompt, y_sample, new_lat_prompt, new_krope_prompt, new_pool_prompt,
            new_lat_sample, new_krope_sample, new_pool_sample)
```

```python
from functools import partial

import jax
import jax.numpy as jnp
from jax import lax
from jax.experimental import pallas as pl
from jax.experimental.pallas import tpu as pltpu

D_MODEL = 2048
N_META = 16
N_HEADS = 12
QK_NOPE = 128
QK_ROPE = 64
V_DIM = 128
Q_RANK = 512
KV_RANK = 512
MLA_WIDTH = N_HEADS * V_DIM
POOL_WIDTH = D_MODEL - MLA_WIDTH
POOL_WINDOWS = (2, 4, 8, 16)
POOL_GROUP = 128
POOL_HIST = 15
D_FF = 4 * D_MODEL
ROPE_THETA = 10000.0
NORM_EPS = 1e-6
PAST_LEN = 16384
PAGE_SIZE = 128

LANES = 128
HEAD_PAD = 256
QCAT = 640
IN_COLS_R = Q_RANK + KV_RANK + 2 * QK_ROPE + POOL_WIDTH
SCALE = (QK_NOPE + QK_ROPE) ** -0.5
VMEM_LIMIT = 56 * 1024 * 1024
TM_MIX = 256
TQ_ATTN = 256
TM_MLP = 512
TF_MLP = 1024
DECODE_GROUP_PAGES = 16
DECODE_SLOTS = 4

BF16 = jnp.bfloat16
F32 = jnp.float32


def _rms(x, g):
    return x * lax.rsqrt(jnp.mean(x * x, axis=-1, keepdims=True) + NORM_EPS) * g


def _rope128(x, c, s):
    return x * c + pltpu.roll(x, QK_ROPE, axis=1) * s


def _const_spec(shape):
    n = len(shape)
    return pl.BlockSpec(shape, lambda *_: (0,) * n, pipeline_mode=pl.Buffered(1))


def _mixer_in_kernel(x_ref, ga_ref, win_ref, gq_ref, wq_ref, gkv_ref, wk_ref, wv_ref,
                     c_ref, s_ref, lat_ref, kpe_ref, pool_ref, q_ref, k_ref, v_ref):
    h = _rms(x_ref[...], ga_ref[...]).astype(BF16)
    proj = jnp.dot(h, win_ref[...], preferred_element_type=F32)
    cos = c_ref[...]
    sin = s_ref[...]
    ckv = _rms(proj[:, Q_RANK:Q_RANK + KV_RANK], gkv_ref[...])
    lat_ref[...] = ckv
    kpe = _rope128(proj[:, 2 * Q_RANK:2 * Q_RANK + LANES], cos, sin)
    kpe_ref[...] = kpe
    pool_ref[...] = proj[:, 2 * Q_RANK + LANES:]
    kpe_b = kpe.astype(BF16)

    qn = _rms(proj[:, :Q_RANK], gq_ref[...]).astype(BF16)
    q = jnp.dot(qn, wq_ref[...], preferred_element_type=F32)
    for hh in range(N_HEADS):
        base = hh * HEAD_PAD
        q_ref[hh, :, :LANES] = (q[:, base:base + LANES] * SCALE).astype(BF16)
        pe = _rope128(q[:, base + LANES:base + HEAD_PAD], cos, sin) * SCALE
        q_ref[hh, :, LANES:] = pe.astype(BF16)

    cb = ckv.astype(BF16)
    kn = jnp.dot(cb, wk_ref[...], preferred_element_type=F32)
    vv = jnp.dot(cb, wv_ref[...], preferred_element_type=F32)
    for hh in range(N_HEADS):
        k_ref[hh, :, :LANES] = kn[:, hh * LANES:(hh + 1) * LANES].astype(BF16)
        k_ref[hh, :, LANES:] = kpe_b
        v_ref[hh] = vv[:, hh * LANES:(hh + 1) * LANES].astype(BF16)


def _mixer_in(x, cos, sin, ga, win, gq, wq, gkv, wk, wv, tm):
    t = x.shape[0]
    table_blocks = cos.shape[0] // tm
    assert t % tm == 0 and cos.shape[0] % tm == 0
    row = lambda w: pl.BlockSpec((tm, w), lambda i: (i, 0))
    table = pl.BlockSpec((tm, LANES), lambda i: (i % table_blocks, 0))
    head = lambda w: pl.BlockSpec((N_HEADS, tm, w), lambda i: (0, i, 0))
    return pl.pallas_call(
        _mixer_in_kernel,
        grid=(t // tm,),
        in_specs=[row(D_MODEL), _const_spec((1, D_MODEL)), _const_spec((D_MODEL, IN_COLS_R)),
                  _const_spec((1, Q_RANK)), _const_spec((Q_RANK, N_HEADS * HEAD_PAD)),
                  _const_spec((1, KV_RANK)), _const_spec((KV_RANK, MLA_WIDTH)),
                  _const_spec((KV_RANK, MLA_WIDTH)), table, table],
        out_specs=[row(KV_RANK), row(LANES), row(POOL_WIDTH),
                   head(HEAD_PAD), head(HEAD_PAD), head(V_DIM)],
        out_shape=[jax.ShapeDtypeStruct((t, KV_RANK), F32),
                   jax.ShapeDtypeStruct((t, LANES), F32),
                   jax.ShapeDtypeStruct((t, POOL_WIDTH), F32),
                   jax.ShapeDtypeStruct((N_HEADS, t, HEAD_PAD), BF16),
                   jax.ShapeDtypeStruct((N_HEADS, t, HEAD_PAD), BF16),
                   jax.ShapeDtypeStruct((N_HEADS, t, V_DIM), BF16)],
        compiler_params=pltpu.CompilerParams(
            dimension_semantics=("arbitrary",), vmem_limit_bytes=VMEM_LIMIT),
        name="mixer_in",
    )(x, ga, win, gq, wq, gkv, wk, wv, cos, sin)


_NT = (((1,), (1,)), ((), ()))


def _attn_tiles(q_ref, k_ref, v_ref, km_ref, vm_ref, o_ref, tiles, tq):
    km = km_ref[0]
    vm = vm_ref[0]
    row = lax.broadcasted_iota(jnp.int32, (tq, tq), 0)
    col = lax.broadcasted_iota(jnp.int32, (tq, tq), 1)
    for ti in tiles:
        r0 = ti * tq
        q = q_ref[0, r0:r0 + tq, :]
        s_meta = lax.dot_general(q, km, _NT, preferred_element_type=F32)
        s_diag = lax.dot_general(q, k_ref[0, r0:r0 + tq, :], _NT, preferred_element_type=F32)
        s_diag = jnp.where(col <= row, s_diag, -jnp.inf)
        m = jnp.maximum(jnp.max(s_diag, axis=-1, keepdims=True),
                        jnp.max(s_meta, axis=-1, keepdims=True))
        if r0:
            s_left = lax.dot_general(q, k_ref[0, :r0, :], _NT, preferred_element_type=F32)
            m = jnp.maximum(m, jnp.max(s_left, axis=-1, keepdims=True))
            p_left = jnp.exp(s_left - m)
        p_diag = jnp.exp(s_diag - m)
        p_meta = jnp.exp(s_meta - m)
        l = jnp.sum(p_diag, axis=-1, keepdims=True) + jnp.sum(p_meta, axis=-1, keepdims=True)
        o = (jnp.dot(p_diag.astype(BF16), v_ref[0, r0:r0 + tq, :], preferred_element_type=F32)
             + jnp.dot(p_meta.astype(BF16), vm, preferred_element_type=F32))
        if r0:
            l = l + jnp.sum(p_left, axis=-1, keepdims=True)
            o = o + jnp.dot(p_left.astype(BF16), v_ref[0, :r0, :], preferred_element_type=F32)
        o_ref[r0:r0 + tq, :] = (o / l).astype(BF16)


def _balanced_halves(n_tiles):
    assert n_tiles % 4 == 0
    pairs = [(i, n_tiles - 1 - i) for i in range(n_tiles // 2)]
    return (tuple(t for p in pairs[0::2] for t in p), tuple(t for p in pairs[1::2] for t in p))


def _mix_project(x, attn_o, mixed_fn, wpool_ref, pscale_ref, wo_ref, gm_ref, h_ref, hn_ref):
    parts = [attn_o]
    for g in range(len(POOL_WINDOWS)):
        lo = g * POOL_GROUP
        y = jnp.dot(mixed_fn(g).astype(BF16), wpool_ref[g], preferred_element_type=F32)
        parts.append((y * pscale_ref[:, lo:lo + POOL_GROUP]).astype(BF16))
    h = x + jnp.dot(jnp.concatenate(parts, axis=1), wo_ref[...], preferred_element_type=F32)
    h_ref[...] = h
    hn_ref[...] = _rms(h, gm_ref[...]).astype(BF16)


def _mix_out_kernel(x_ref, o_ref, pool_ref, halo_ref, pmeta_ref, wpool_ref, pscale_ref, wo_ref,
                    gm_ref, h_ref, hn_ref, ext_ref, *, tm, tiles_per_seq):
    i = pl.program_id(0)
    first = (i % tiles_per_seq) == 0
    ext_ref[0:N_META, :] = jnp.where(first, pmeta_ref[...], halo_ref[...])
    ext_ref[N_META:, :] = pool_ref[...]

    def mixed(g):
        w = POOL_WINDOWS[g]
        lo = g * POOL_GROUP
        cur = ext_ref[N_META:N_META + tm, lo:lo + POOL_GROUP]
        tot = cur
        for kk in range(1, w):
            tot = tot + ext_ref[N_META - kk:N_META - kk + tm, lo:lo + POOL_GROUP]
        return tot / float(w) - cur

    _mix_project(x_ref[...], o_ref[...], mixed, wpool_ref, pscale_ref, wo_ref, gm_ref,
                 h_ref, hn_ref)


def _mix_out(x, o, pool, pool_meta, wpool, pscale, wo, gm, tm, seq):
    t = x.shape[0]
    assert seq % tm == 0 and tm % N_META == 0 and N_META > POOL_HIST
    row = lambda w: pl.BlockSpec((tm, w), lambda i: (i, 0))
    hb = tm // N_META
    return pl.pallas_call(
        partial(_mix_out_kernel, tm=tm, tiles_per_seq=seq // tm),
        grid=(t // tm,),
        in_specs=[row(D_MODEL), row(MLA_WIDTH), row(POOL_WIDTH),
                  pl.BlockSpec((N_META, POOL_WIDTH), lambda i: (jnp.maximum(i * hb - 1, 0), 0)),
                  _const_spec((N_META, POOL_WIDTH)),
                  _const_spec((len(POOL_WINDOWS), POOL_GROUP, POOL_GROUP)),
                  _const_spec((1, POOL_WIDTH)), _const_spec((D_MODEL, D_MODEL)),
                  _const_spec((1, D_MODEL))],
        out_specs=[row(D_MODEL), row(D_MODEL)],
        out_shape=[jax.ShapeDtypeStruct((t, D_MODEL), F32),
                   jax.ShapeDtypeStruct((t, D_MODEL), BF16)],
        scratch_shapes=[pltpu.VMEM((tm + N_META, POOL_WIDTH), F32)],
        compiler_params=pltpu.CompilerParams(
            dimension_semantics=("arbitrary",), vmem_limit_bytes=VMEM_LIMIT),
        name="mix_out",
    )(x, o, pool, pool, pool_meta, wpool, pscale, wo, gm)


def _mix_out_sample_kernel(x_ref, o_ref, pin_ref, st_ref, wpool_ref, pscale_ref, wo_ref, gm_ref,
                           h_ref, hn_ref):
    def mixed(g):
        w = POOL_WINDOWS[g]
        lo = g * POOL_GROUP
        cur = pin_ref[:, lo:lo + POOL_GROUP]
        tot = cur
        for kk in range(1, w):
            base = (POOL_HIST - kk) * POOL_WIDTH + lo
            tot = tot + st_ref[:, base:base + POOL_GROUP]
        return tot / float(w) - cur

    _mix_project(x_ref[...], o_ref[...].astype(BF16), mixed, wpool_ref, pscale_ref, wo_ref,
                 gm_ref, h_ref, hn_ref)


def _mix_out_sample(x, o, pin, state, wpool, pscale, wo, gm):
    t = x.shape[0]
    full = lambda a: pl.BlockSpec(a.shape, lambda: (0,) * a.ndim)
    args = (x, o, pin, state, wpool, pscale, wo, gm)
    return pl.pallas_call(
        _mix_out_sample_kernel,
        in_specs=[full(a) for a in args],
        out_specs=[pl.BlockSpec((t, D_MODEL), lambda: (0, 0))] * 2,
        out_shape=[jax.ShapeDtypeStruct((t, D_MODEL), F32),
                   jax.ShapeDtypeStruct((t, D_MODEL), BF16)],
        compiler_params=pltpu.CompilerParams(vmem_limit_bytes=VMEM_LIMIT),
        name="mix_out_sample",
    )(*args)


def _mlp_kernel(h_ref, hn_ref, wu_ref, wd_ref, gf_ref, y_ref):
    f = pl.program_id(1)

    @pl.when(f == 0)
    def _():
        y_ref[...] = h_ref[...]

    z = jnp.dot(hn_ref[...], wu_ref[...], preferred_element_type=F32)
    a = jnp.square(jnp.maximum(z, 0.0)).astype(BF16)
    y_ref[...] += jnp.dot(a, wd_ref[...], preferred_element_type=F32)

    @pl.when(f == pl.num_programs(1) - 1)
    def _():
        y_ref[...] = _rms(y_ref[...], gf_ref[...])


def _mlp(h, hn, wu, wd, gf, tm, tf):
    t = h.shape[0]
    assert t % tm == 0 and D_FF % tf == 0
    return pl.pallas_call(
        _mlp_kernel,
        grid=(t // tm, D_FF // tf),
        in_specs=[pl.BlockSpec((tm, D_MODEL), lambda i, f: (i, 0)),
                  pl.BlockSpec((tm, D_MODEL), lambda i, f: (i, 0)),
                  pl.BlockSpec((D_MODEL, tf), lambda i, f: (0, f)),
                  pl.BlockSpec((tf, D_MODEL), lambda i, f: (f, 0)),
                  pl.BlockSpec((1, D_MODEL), lambda i, f: (0, 0))],
        out_specs=pl.BlockSpec((tm, D_MODEL), lambda i, f: (i, 0)),
        out_shape=jax.ShapeDtypeStruct((t, D_MODEL), F32),
        compiler_params=pltpu.CompilerParams(
            dimension_semantics=("arbitrary", "arbitrary"), vmem_limit_bytes=VMEM_LIMIT),
        name="mlp",
    )(h, hn, wu, wd, gf)


def _absorb_kernel(q_ref, wk_ref, o_ref):
    q = q_ref[0]
    qlat = lax.dot_general(q[:, :LANES], wk_ref[...], _NT,
                           preferred_element_type=F32)
    o_ref[:, :KV_RANK] = qlat
    o_ref[:, KV_RANK:] = q[:, LANES:].astype(F32)


def _absorb(q, wk, db):
    return pl.pallas_call(
        _absorb_kernel,
        grid=(N_HEADS,),
        in_specs=[pl.BlockSpec((1, db, HEAD_PAD), lambda h: (h, 0, 0)),
                  pl.BlockSpec((KV_RANK, LANES), lambda h: (0, h))],
        out_specs=pl.BlockSpec((db, QCAT), lambda h: (0, h)),
        out_shape=jax.ShapeDtypeStruct((db, N_HEADS * QCAT), F32),
        compiler_params=pltpu.CompilerParams(dimension_semantics=("arbitrary",)),
        name="absorb_q",
    )(q, wk)


def _decode_attend_kernel(pt_ref, q_ref, cown_ref, kown_ref, lat_hbm, krt_hbm,
                          qa_ref, ka_ref, va_ref, km_ref, vm_ref, o_ref, oa_ref,
                          latbuf, krtbuf, sem, *, gp, ng, nslot, tq, halves, attn_steps):
    b = pl.program_id(0)
    nb = pl.num_programs(0)

    def group_copies(bb, gi):
        slot = gi % nslot
        cps = []
        for k in range(gp):
            page = pt_ref[bb, gi * gp + k]
            cps.append(pltpu.make_async_copy(lat_hbm.at[page], latbuf.at[slot, k], sem.at[slot]))
            cps.append(pltpu.make_async_copy(krt_hbm.at[page], krtbuf.at[slot, k], sem.at[slot]))
        return cps

    def start(bb, gi):
        for cp in group_copies(bb, gi):
            cp.start()

    def wait(bb, gi):
        for cp in group_copies(bb, gi):
            cp.wait()

    @pl.when(b == 0)
    def _():
        for gi in range(nslot):
            start(0, gi)

    q = q_ref[0]
    qlat = q[:, :KV_RANK]
    qpe = q[:, KV_RANK:KV_RANK + QK_ROPE]

    def scores(gi):
        slot = gi % nslot
        parts = [lax.dot_general(qlat, latbuf[slot, k], _NT, preferred_element_type=F32)
                 + jnp.dot(qpe, krtbuf[slot, k], preferred_element_type=F32)
                 for k in range(gp)]
        return jnp.concatenate(parts, axis=-1)

    def fold(gi, s, m, l, acc):
        slot = gi % nslot
        m_new = jnp.maximum(m, jnp.max(s, axis=-1, keepdims=True))
        corr = jnp.exp(m - m_new)
        p = jnp.exp(s - m_new)
        l = l * corr + jnp.sum(p, axis=-1, keepdims=True)
        acc = acc * corr
        for k in range(gp):
            acc = acc + jnp.dot(p[:, k * PAGE_SIZE:(k + 1) * PAGE_SIZE], latbuf[slot, k],
                                preferred_element_type=F32)
        return m_new, l, acc

    c_own = cown_ref[0]
    k_own = kown_ref[0]
    m = (jnp.sum(qlat * c_own, axis=-1, keepdims=True)
         + jnp.sum(qpe * k_own[:, :QK_ROPE], axis=-1, keepdims=True))
    l = jnp.ones_like(m)
    acc = jnp.broadcast_to(c_own, (N_HEADS, KV_RANK))

    wait(b, 0)
    s_cur = scores(0)
    for gi in range(ng):
        if gi + 1 < ng:
            wait(b, gi + 1)
            s_next = scores(gi + 1)
        m, l, acc = fold(gi, s_cur, m, l, acc)
        nxt = gi + nslot
        if nxt < ng:
            start(b, nxt)
        else:
            @pl.when(b + 1 < nb)
            def _(nxt=nxt):
                start(b + 1, nxt - ng)
        if gi + 1 < ng:
            s_cur = s_next
    o_ref[0] = acc / l

    for half, tiles in enumerate(halves):
        @pl.when((b < attn_steps) & (b % len(halves) == half))
        def _(tiles=tiles):
            _attn_tiles(qa_ref, ka_ref, va_ref, km_ref, vm_ref, oa_ref, tiles, tq)


def _decode_attend(page_table, qcat, c_own, k_own, cache_lat, cache_krt,
                   q_x, k_x, v_x, k_meta, v_meta, batch, seq):
    db, n_pages = page_table.shape
    gp, nslot, tq = DECODE_GROUP_PAGES, DECODE_SLOTS, TQ_ATTN
    ng = n_pages // gp
    assert n_pages % gp == 0 and ng % nslot == 0 and seq % tq == 0
    n_units = batch * N_HEADS
    halves = _balanced_halves(seq // tq)
    attn_steps = len(halves) * n_units
    assert attn_steps <= db

    def unit(b):
        u = jnp.minimum(b // len(halves), n_units - 1)
        return u % N_HEADS, u // N_HEADS

    per_seq = lambda shape: pl.BlockSpec(shape, lambda b, pt: (b, 0, 0))
    a_seq = lambda w: pl.BlockSpec((1, seq, w), lambda b, pt: (*unit(b), 0))
    a_meta = lambda w: pl.BlockSpec((1, N_META, w), lambda b, pt: (unit(b)[0], 0, 0))
    grid_spec = pltpu.PrefetchScalarGridSpec(
        num_scalar_prefetch=1,
        grid=(db,),
        in_specs=[per_seq((1, N_HEADS, QCAT)), per_seq((1, 1, KV_RANK)), per_seq((1, 1, LANES)),
                  pl.BlockSpec(memory_space=pl.ANY), pl.BlockSpec(memory_space=pl.ANY),
                  a_seq(HEAD_PAD), a_seq(HEAD_PAD), a_seq(V_DIM), a_meta(HEAD_PAD), a_meta(V_DIM)],
        out_specs=[per_seq((1, N_HEADS, KV_RANK)),
                   pl.BlockSpec((seq, V_DIM), lambda b, pt: unit(b)[::-1])],
        scratch_shapes=[pltpu.VMEM((nslot, gp, PAGE_SIZE, KV_RANK), F32),
                        pltpu.VMEM((nslot, gp, QK_ROPE, PAGE_SIZE), F32),
                        pltpu.SemaphoreType.DMA((nslot,))],
    )
    return pl.pallas_call(
        partial(_decode_attend_kernel, gp=gp, ng=ng, nslot=nslot, tq=tq, halves=halves,
                attn_steps=attn_steps),
        grid_spec=grid_spec,
        out_shape=[jax.ShapeDtypeStruct((db, N_HEADS, KV_RANK), F32),
                   jax.ShapeDtypeStruct((batch * seq, MLA_WIDTH), BF16)],
        compiler_params=pltpu.CompilerParams(
            dimension_semantics=("arbitrary",), vmem_limit_bytes=VMEM_LIMIT),
        name="paged_decode_prompt_attn",
    )(page_table, qcat, c_own, k_own, cache_lat, cache_krt, q_x, k_x, v_x, k_meta, v_meta)


def _unabsorb_kernel(lat_ref, wv_ref, o_ref):
    o_ref[...] = jnp.dot(lat_ref[...].astype(BF16), wv_ref[...], preferred_element_type=F32)


def _unabsorb(lat, wv):
    db = lat.shape[0]
    return pl.pallas_call(
        _unabsorb_kernel,
        grid=(N_HEADS,),
        in_specs=[pl.BlockSpec((db, KV_RANK), lambda h: (0, h)),
                  pl.BlockSpec((KV_RANK, V_DIM), lambda h: (0, h))],
        out_specs=pl.BlockSpec((db, V_DIM), lambda h: (0, h)),
        out_shape=jax.ShapeDtypeStruct((db, MLA_WIDTH), F32),
        compiler_params=pltpu.CompilerParams(dimension_semantics=("arbitrary",)),
        name="unabsorb_v",
    )(lat, wv)


def _rot_cols(w):
    half = w.shape[-1] // 2
    return jnp.concatenate([-w[..., half:], w[..., :half]], axis=-1)


def _rope_tables(pos):
    inv = ROPE_THETA ** (-jnp.arange(0, QK_ROPE, 2, dtype=F32) / QK_ROPE)
    ang = pos.astype(F32)[:, None] * inv[None, :]
    ang = jnp.concatenate([ang, ang], axis=-1)
    z = jnp.zeros_like(ang)
    return (jnp.concatenate([jnp.cos(ang), z], axis=-1),
            jnp.concatenate([jnp.sin(ang), z], axis=-1))


def kernel(x_prompt, x_sample, cache_kv_latent, cache_k_rope, state_pool, page_table, meta_tokens,
           g_attn, w_in, g_q, w_q_b, g_kv, w_kv_b, w_pool, pool_scale, w_o, g_mlp, w_up, w_down,
           g_final):
    assert g_attn.shape[0] == 1, "single-layer trunk"
    batch, seq, _ = x_prompt.shape
    db, s1, _ = x_sample.shape
    assert s1 == 1
    assert seq >= POOL_HIST

    kr_lo = Q_RANK + KV_RANK
    win = w_in[0]
    win_r = jnp.concatenate([win[:, :kr_lo + QK_ROPE], _rot_cols(win[:, kr_lo:kr_lo + QK_ROPE]),
                             win[:, kr_lo + QK_ROPE:]], axis=1).astype(BF16)
    wq = w_q_b[0].reshape(Q_RANK, N_HEADS, QK_NOPE + QK_ROPE)
    wq_r = jnp.concatenate([wq, _rot_cols(wq[..., QK_NOPE:])], axis=-1)
    wq_r = wq_r.reshape(Q_RANK, N_HEADS * HEAD_PAD).astype(BF16)
    wk = w_kv_b[0][..., :QK_NOPE].reshape(KV_RANK, MLA_WIDTH).astype(BF16)
    wv = w_kv_b[0][..., QK_NOPE:].reshape(KV_RANK, MLA_WIDTH).astype(BF16)
    wpool = w_pool[0].astype(BF16)
    wo = w_o[0].astype(BF16)
    wu = w_up[0].astype(BF16)
    wd = w_down[0].astype(BF16)
    ga, gq, gkv, gm = g_attn, g_q, g_kv, g_mlp
    gf = g_final[None, :]
    pscale = pool_scale

    mix_w = (ga, win_r, gq, wq_r, gkv, wk, wv)
    x = x_prompt.reshape(batch * seq, D_MODEL)
    cos_x, sin_x = _rope_tables(N_META + jnp.arange(seq, dtype=jnp.int32))
    lat_x, kpe_x, pool_x, q_x, k_x, v_x = _mixer_in(x, cos_x, sin_x, *mix_w, tm=TM_MIX)
    xs = x_sample.reshape(db, D_MODEL)
    x_sm = jnp.concatenate([xs, meta_tokens], axis=0)
    pos_sm = jnp.concatenate([jnp.full((db,), PAST_LEN, jnp.int32),
                              jnp.arange(N_META, dtype=jnp.int32)])
    cos_sm, sin_sm = _rope_tables(pos_sm)
    lat_sm, kpe_sm, pool_sm, q_sm, k_sm, v_sm = _mixer_in(x_sm, cos_sm, sin_sm, *mix_w,
                                                          tm=db + N_META)
    lat_s, kpe_s, pool_s = lat_sm[:db], kpe_sm[:db], pool_sm[:db]
    lat_m, kpe_m, pool_m = lat_sm[db:], kpe_sm[db:], pool_sm[db:]

    qcat = _absorb(q_sm, wk, db).reshape(db, N_HEADS, QCAT)
    n_phys = cache_kv_latent.shape[1]
    dec, o_x = _decode_attend(
        page_table, qcat, lat_s.reshape(db, 1, KV_RANK), kpe_s.reshape(db, 1, LANES),
        cache_kv_latent.reshape(n_phys, PAGE_SIZE, KV_RANK),
        jnp.swapaxes(cache_k_rope[0], 1, 2),
        q_x, k_x, v_x, k_sm[:, db:], v_sm[:, db:], batch, seq)

    h_x, hn_x = _mix_out(x, o_x, pool_x, pool_m, wpool, pscale, wo, gm, tm=TM_MIX, seq=seq)
    y_x = _mlp(h_x, hn_x, wu, wd, gf, tm=TM_MLP, tf=TF_MLP)

    per_seq = lambda a: a.reshape(batch, seq, a.shape[-1])
    with_meta = lambda m, a: jnp.concatenate(
        [jnp.broadcast_to(m[None], (batch,) + m.shape), per_seq(a)], axis=1)[None]
    y_prompt = per_seq(y_x)
    new_lat_prompt = with_meta(lat_m, lat_x)
    new_krope_prompt = with_meta(kpe_m[:, :QK_ROPE], kpe_x[:, :QK_ROPE])
    new_pool_prompt = per_seq(pool_x)[None, :, seq - POOL_HIST:]

    a_s = _unabsorb(dec.reshape(db, N_HEADS * KV_RANK), wv)
    state = state_pool[0]
    h_s, hn_s = _mix_out_sample(xs, a_s, pool_s, state.reshape(db, POOL_HIST * POOL_WIDTH),
                                wpool, pscale, wo, gm)
    y_s = _mlp(h_s, hn_s, wu, wd, gf, tm=db, tf=TF_MLP)

    y_sample = y_s.reshape(db, 1, D_MODEL)
    new_lat_sample = lat_s.reshape(1, db, 1, KV_RANK)
    new_krope_sample = kpe_s[:, :QK_ROPE].reshape(1, db, 1, QK_ROPE)
    new_pool_sample = jnp.concatenate([state[:, 1:], pool_s[:, None, :]], axis=1)[None]
    return (y_prompt, y_sample, new_lat_prompt, new_krope_prompt, new_pool_prompt,
            new_lat_sample, new_krope_sample, new_pool_sample)
```

```python
from functools import partial

import jax
import jax.numpy as jnp
import numpy as np
from jax import lax
from jax.experimental import pallas as pl
from jax.experimental.pallas import tpu as pltpu

D_MODEL = 2048
N_META = 16
N_HEADS = 12
QK_NOPE = 128
QK_ROPE = 64
V_DIM = 128
Q_RANK = 512
KV_RANK = 512
MLA_WIDTH = N_HEADS * V_DIM
POOL_WIDTH = D_MODEL - MLA_WIDTH
POOL_WINDOWS = (2, 4, 8, 16)
POOL_GROUP = 128
POOL_HIST = 15
D_FF = 4 * D_MODEL
ROPE_THETA = 10000.0
NORM_EPS = 1e-6
PAST_LEN = 16384
PAGE_SIZE = 128

LANES = 128
HEAD_PAD = 256
QCAT = 640
IN_COLS_R = Q_RANK + KV_RANK + 2 * QK_ROPE + POOL_WIDTH
SCALE = (QK_NOPE + QK_ROPE) ** -0.5
VMEM_LIMIT = 56 * 1024 * 1024
TM_MIX = 256
TM_OUT = 512
TQ_ATTN = 256
TM_MLP = 512
TF_MLP = 1024
DECODE_GROUP_PAGES = 16
DECODE_SLOTS = 4

BF16 = jnp.bfloat16
F32 = jnp.float32


def _rms(x, g):
    return x * lax.rsqrt(jnp.mean(x * x, axis=-1, keepdims=True) + NORM_EPS) * g


def _rope128(x, c, s):
    return x * c + pltpu.roll(x, QK_ROPE, axis=1) * s


def _const_spec(shape):
    n = len(shape)
    return pl.BlockSpec(shape, lambda *_: (0,) * n, pipeline_mode=pl.Buffered(1))


def _mixer_in_kernel(x_ref, ga_ref, win_ref, gq_ref, wq_ref, gkv_ref, wk_ref, wv_ref,
                     c_ref, s_ref, lat_ref, kpe_ref, pool_ref, q_ref, k_ref, v_ref):
    h = _rms(x_ref[...], ga_ref[...]).astype(BF16)
    proj = jnp.dot(h, win_ref[...], preferred_element_type=F32)
    cos = c_ref[...]
    sin = s_ref[...]
    ckv = _rms(proj[:, Q_RANK:Q_RANK + KV_RANK], gkv_ref[...])
    lat_ref[...] = ckv
    kpe = _rope128(proj[:, 2 * Q_RANK:2 * Q_RANK + LANES], cos, sin)
    kpe_ref[...] = kpe
    pool_ref[...] = proj[:, 2 * Q_RANK + LANES:]
    kpe_b = kpe.astype(BF16)

    qn = _rms(proj[:, :Q_RANK], gq_ref[...]).astype(BF16)
    q = jnp.dot(qn, wq_ref[...], preferred_element_type=F32)
    for hh in range(N_HEADS):
        base = hh * HEAD_PAD
        q_ref[hh, :, :LANES] = (q[:, base:base + LANES] * SCALE).astype(BF16)
        pe = _rope128(q[:, base + LANES:base + HEAD_PAD], cos, sin) * SCALE
        q_ref[hh, :, LANES:] = pe.astype(BF16)

    cb = ckv.astype(BF16)
    kn = jnp.dot(cb, wk_ref[...], preferred_element_type=F32)
    vv = jnp.dot(cb, wv_ref[...], preferred_element_type=F32)
    for hh in range(N_HEADS):
        k_ref[hh, :, :LANES] = kn[:, hh * LANES:(hh + 1) * LANES].astype(BF16)
        k_ref[hh, :, LANES:] = kpe_b
        v_ref[hh] = vv[:, hh * LANES:(hh + 1) * LANES].astype(BF16)


def _mixer_in(x, cos, sin, ga, win, gq, wq, gkv, wk, wv, tm):
    t = x.shape[0]
    table_blocks = cos.shape[0] // tm
    assert t % tm == 0 and cos.shape[0] % tm == 0
    row = lambda w: pl.BlockSpec((tm, w), lambda i: (i, 0))
    table = pl.BlockSpec((tm, LANES), lambda i: (i % table_blocks, 0))
    head = lambda w: pl.BlockSpec((N_HEADS, tm, w), lambda i: (0, i, 0))
    return pl.pallas_call(
        _mixer_in_kernel,
        grid=(t // tm,),
        in_specs=[row(D_MODEL), _const_spec((1, D_MODEL)), _const_spec((D_MODEL, IN_COLS_R)),
                  _const_spec((1, Q_RANK)), _const_spec((Q_RANK, N_HEADS * HEAD_PAD)),
                  _const_spec((1, KV_RANK)), _const_spec((KV_RANK, MLA_WIDTH)),
                  _const_spec((KV_RANK, MLA_WIDTH)), table, table],
        out_specs=[row(KV_RANK), row(LANES), row(POOL_WIDTH),
                   head(HEAD_PAD), head(HEAD_PAD), head(V_DIM)],
        out_shape=[jax.ShapeDtypeStruct((t, KV_RANK), F32),
                   jax.ShapeDtypeStruct((t, LANES), F32),
                   jax.ShapeDtypeStruct((t, POOL_WIDTH), F32),
                   jax.ShapeDtypeStruct((N_HEADS, t, HEAD_PAD), BF16),
                   jax.ShapeDtypeStruct((N_HEADS, t, HEAD_PAD), BF16),
                   jax.ShapeDtypeStruct((N_HEADS, t, V_DIM), BF16)],
        compiler_params=pltpu.CompilerParams(
            dimension_semantics=("arbitrary",), vmem_limit_bytes=VMEM_LIMIT),
        name="mixer_in",
    )(x, ga, win, gq, wq, gkv, wk, wv, cos, sin)


_NT = (((1,), (1,)), ((), ()))


def _attn_tiles(q_ref, k_ref, v_ref, km_ref, vm_ref, o_ref, tiles, tq):
    km = km_ref[0]
    vm = vm_ref[0]
    row = lax.broadcasted_iota(jnp.int32, (tq, tq), 0)
    col = lax.broadcasted_iota(jnp.int32, (tq, tq), 1)
    for ti in tiles:
        r0 = ti * tq
        q = q_ref[0, r0:r0 + tq, :]
        s_meta = lax.dot_general(q, km, _NT, preferred_element_type=F32)
        s_diag = lax.dot_general(q, k_ref[0, r0:r0 + tq, :], _NT, preferred_element_type=F32)
        s_diag = jnp.where(col <= row, s_diag, -jnp.inf)
        m = jnp.maximum(jnp.max(s_diag, axis=-1, keepdims=True),
                        jnp.max(s_meta, axis=-1, keepdims=True))
        if r0:
            s_left = lax.dot_general(q, k_ref[0, :r0, :], _NT, preferred_element_type=F32)
            m = jnp.maximum(m, jnp.max(s_left, axis=-1, keepdims=True))
            p_left = jnp.exp(s_left - m)
        p_diag = jnp.exp(s_diag - m)
        p_meta = jnp.exp(s_meta - m)
        l = jnp.sum(p_diag, axis=-1, keepdims=True) + jnp.sum(p_meta, axis=-1, keepdims=True)
        o = (jnp.dot(p_diag.astype(BF16), v_ref[0, r0:r0 + tq, :], preferred_element_type=F32)
             + jnp.dot(p_meta.astype(BF16), vm, preferred_element_type=F32))
        if r0:
            l = l + jnp.sum(p_left, axis=-1, keepdims=True)
            o = o + jnp.dot(p_left.astype(BF16), v_ref[0, :r0, :], preferred_element_type=F32)
        o_ref[r0:r0 + tq, :] = (o / l).astype(BF16)


def _balanced_halves(n_tiles):
    assert n_tiles % 4 == 0
    pairs = [(i, n_tiles - 1 - i) for i in range(n_tiles // 2)]
    return (tuple(t for p in pairs[0::2] for t in p), tuple(t for p in pairs[1::2] for t in p))


def _mix_project(x, attn_o, mixed_fn, wpool_ref, pscale_ref, wo_ref, gm_ref, h_ref, hn_ref):
    parts = [attn_o]
    for g in range(len(POOL_WINDOWS)):
        lo = g * POOL_GROUP
        y = jnp.dot(mixed_fn(g).astype(BF16), wpool_ref[g], preferred_element_type=F32)
        parts.append((y * pscale_ref[:, lo:lo + POOL_GROUP]).astype(BF16))
    h = x + jnp.dot(jnp.concatenate(parts, axis=1), wo_ref[...], preferred_element_type=F32)
    h_ref[...] = h
    hn_ref[...] = _rms(h, gm_ref[...]).astype(BF16)


def _mix_out_kernel(x_ref, o_ref, pool_ref, halo_ref, pmeta_ref, wpool_ref, pscale_ref, wo_ref,
                    gm_ref, h_ref, hn_ref, ext_ref, *, tm, tiles_per_seq):
    i = pl.program_id(0)
    first = (i % tiles_per_seq) == 0
    ext_ref[0:N_META, :] = jnp.where(first, pmeta_ref[...], halo_ref[...])
    ext_ref[N_META:, :] = pool_ref[...]

    def mixed(g):
        w = POOL_WINDOWS[g]
        lo = g * POOL_GROUP
        cur = ext_ref[N_META:N_META + tm, lo:lo + POOL_GROUP]
        tot = cur
        for kk in range(1, w):
            tot = tot + ext_ref[N_META - kk:N_META - kk + tm, lo:lo + POOL_GROUP]
        return tot / float(w) - cur

    _mix_project(x_ref[...], o_ref[...], mixed, wpool_ref, pscale_ref, wo_ref, gm_ref,
                 h_ref, hn_ref)


def _mix_out(x, o, pool, pool_meta, wpool, pscale, wo, gm, tm, seq):
    t = x.shape[0]
    assert seq % tm == 0 and tm % N_META == 0 and N_META > POOL_HIST
    row = lambda w: pl.BlockSpec((tm, w), lambda i: (i, 0))
    hb = tm // N_META
    return pl.pallas_call(
        partial(_mix_out_kernel, tm=tm, tiles_per_seq=seq // tm),
        grid=(t // tm,),
        in_specs=[row(D_MODEL), row(MLA_WIDTH), row(POOL_WIDTH),
                  pl.BlockSpec((N_META, POOL_WIDTH), lambda i: (jnp.maximum(i * hb - 1, 0), 0)),
                  _const_spec((N_META, POOL_WIDTH)),
                  _const_spec((len(POOL_WINDOWS), POOL_GROUP, POOL_GROUP)),
                  _const_spec((1, POOL_WIDTH)), _const_spec((D_MODEL, D_MODEL)),
                  _const_spec((1, D_MODEL))],
        out_specs=[row(D_MODEL), row(D_MODEL)],
        out_shape=[jax.ShapeDtypeStruct((t, D_MODEL), F32),
                   jax.ShapeDtypeStruct((t, D_MODEL), BF16)],
        scratch_shapes=[pltpu.VMEM((tm + N_META, POOL_WIDTH), F32)],
        compiler_params=pltpu.CompilerParams(
            dimension_semantics=("arbitrary",), vmem_limit_bytes=VMEM_LIMIT),
        name="mix_out",
    )(x, o, pool, pool, pool_meta, wpool, pscale, wo, gm)


def _mix_out_sample_kernel(x_ref, o_ref, pin_ref, st_ref, wpool_ref, pscale_ref, wo_ref, gm_ref,
                           h_ref, hn_ref):
    def mixed(g):
        w = POOL_WINDOWS[g]
        lo = g * POOL_GROUP
        cur = pin_ref[:, lo:lo + POOL_GROUP]
        tot = cur
        for kk in range(1, w):
            base = (POOL_HIST - kk) * POOL_WIDTH + lo
            tot = tot + st_ref[:, base:base + POOL_GROUP]
        return tot / float(w) - cur

    _mix_project(x_ref[...], o_ref[...].astype(BF16), mixed, wpool_ref, pscale_ref, wo_ref,
                 gm_ref, h_ref, hn_ref)


def _mix_out_sample(x, o, pin, state, wpool, pscale, wo, gm):
    t = x.shape[0]
    full = lambda a: pl.BlockSpec(a.shape, lambda: (0,) * a.ndim)
    args = (x, o, pin, state, wpool, pscale, wo, gm)
    return pl.pallas_call(
        _mix_out_sample_kernel,
        in_specs=[full(a) for a in args],
        out_specs=[pl.BlockSpec((t, D_MODEL), lambda: (0, 0))] * 2,
        out_shape=[jax.ShapeDtypeStruct((t, D_MODEL), F32),
                   jax.ShapeDtypeStruct((t, D_MODEL), BF16)],
        compiler_params=pltpu.CompilerParams(vmem_limit_bytes=VMEM_LIMIT),
        name="mix_out_sample",
    )(*args)


def _mlp_kernel(h_ref, hn_ref, wu_ref, wd_ref, gf_ref, y_ref):
    f = pl.program_id(1)

    @pl.when(f == 0)
    def _():
        y_ref[...] = h_ref[...]

    z = jnp.dot(hn_ref[...], wu_ref[...], preferred_element_type=F32)
    a = jnp.square(jnp.maximum(z, 0.0)).astype(BF16)
    y_ref[...] += jnp.dot(a, wd_ref[...], preferred_element_type=F32)

    @pl.when(f == pl.num_programs(1) - 1)
    def _():
        y_ref[...] = _rms(y_ref[...], gf_ref[...])


def _mlp(h, hn, wu, wd, gf, tm, tf):
    t = h.shape[0]
    assert t % tm == 0 and D_FF % tf == 0
    return pl.pallas_call(
        _mlp_kernel,
        grid=(t // tm, D_FF // tf),
        in_specs=[pl.BlockSpec((tm, D_MODEL), lambda i, f: (i, 0)),
                  pl.BlockSpec((tm, D_MODEL), lambda i, f: (i, 0)),
                  pl.BlockSpec((D_MODEL, tf), lambda i, f: (0, f)),
                  pl.BlockSpec((tf, D_MODEL), lambda i, f: (f, 0)),
                  pl.BlockSpec((1, D_MODEL), lambda i, f: (0, 0))],
        out_specs=pl.BlockSpec((tm, D_MODEL), lambda i, f: (i, 0)),
        out_shape=jax.ShapeDtypeStruct((t, D_MODEL), F32),
        compiler_params=pltpu.CompilerParams(
            dimension_semantics=("arbitrary", "arbitrary"), vmem_limit_bytes=VMEM_LIMIT),
        name="mlp",
    )(h, hn, wu, wd, gf)


def _absorb_kernel(q_ref, wk_ref, o_ref):
    q = q_ref[0]
    qlat = lax.dot_general(q[:, :LANES], wk_ref[...], _NT,
                           preferred_element_type=F32)
    o_ref[:, :KV_RANK] = qlat
    o_ref[:, KV_RANK:] = q[:, LANES:].astype(F32)


def _absorb(q, wk, db):
    return pl.pallas_call(
        _absorb_kernel,
        grid=(N_HEADS,),
        in_specs=[pl.BlockSpec((1, db, HEAD_PAD), lambda h: (h, 0, 0)),
                  pl.BlockSpec((KV_RANK, LANES), lambda h: (0, h))],
        out_specs=pl.BlockSpec((db, QCAT), lambda h: (0, h)),
        out_shape=jax.ShapeDtypeStruct((db, N_HEADS * QCAT), F32),
        compiler_params=pltpu.CompilerParams(dimension_semantics=("arbitrary",)),
        name="absorb_q",
    )(q, wk)


def _decode_attend_kernel(pt_ref, q_ref, cown_ref, kown_ref, lat_hbm, krt_hbm,
                          qa_ref, ka_ref, va_ref, km_ref, vm_ref, o_ref, oa_ref,
                          latbuf, krtbuf, sem, *, gp, ng, nslot, tq, halves, attn_steps):
    b = pl.program_id(0)
    nb = pl.num_programs(0)

    def page_copies(bb, gi, k):
        slot = gi % nslot
        page = pt_ref[bb, gi * gp + k]
        return (pltpu.make_async_copy(lat_hbm.at[page], latbuf.at[slot, k], sem.at[slot]),
                pltpu.make_async_copy(krt_hbm.at[page], krtbuf.at[slot, k], sem.at[slot]))

    def start(bb, gi):
        for k in range(gp):
            for cp in page_copies(bb, gi, k):
                cp.start()

    def wait(bb, gi):
        for k in range(gp):
            for cp in page_copies(bb, gi, k):
                cp.wait()

    def refill_target(gi):
        nxt = gi + nslot
        if nxt < ng:
            return b, nxt
        return jnp.minimum(b + 1, nb - 1), nxt - ng

    @pl.when(b == 0)
    def _():
        for gi in range(nslot):
            start(0, gi)

    q = q_ref[0]
    qlat = q[:, :KV_RANK]
    qpe = q[:, KV_RANK:KV_RANK + QK_ROPE]

    def scores(gi):
        slot = gi % nslot
        parts = [lax.dot_general(qlat, latbuf[slot, k], _NT, preferred_element_type=F32)
                 + jnp.dot(qpe, krtbuf[slot, k], preferred_element_type=F32)
                 for k in range(gp)]
        return jnp.concatenate(parts, axis=-1)

    def fold(gi, s, m, l, acc):
        slot = gi % nslot
        m_new = jnp.maximum(m, jnp.max(s, axis=-1, keepdims=True))
        corr = jnp.exp(m - m_new)
        p = jnp.exp(s - m_new)
        l = l * corr + jnp.sum(p, axis=-1, keepdims=True)
        acc = acc * corr
        for k in range(gp):
            acc = acc + jnp.dot(p[:, k * PAGE_SIZE:(k + 1) * PAGE_SIZE], latbuf[slot, k],
                                preferred_element_type=F32)
        start(*refill_target(gi))
        return m_new, l, acc

    c_own = cown_ref[0]
    k_own = kown_ref[0]
    m = (jnp.sum(qlat * c_own, axis=-1, keepdims=True)
         + jnp.sum(qpe * k_own[:, :QK_ROPE], axis=-1, keepdims=True))
    l = jnp.ones_like(m)
    acc = jnp.broadcast_to(c_own, (N_HEADS, KV_RANK))

    wait(b, 0)
    wait(b, 1)
    s_cur = scores(0)
    for gi in range(ng):
        if gi + 1 < ng:
            if gi % 2 == 1:
                for gw in range(gi + 1, min(gi + 3, ng)):
                    wait(b, gw)
            s_next = scores(gi + 1)
        m, l, acc = fold(gi, s_cur, m, l, acc)
        if gi + 1 < ng:
            s_cur = s_next
    o_ref[0] = acc / l

    for half, tiles in enumerate(halves):
        @pl.when((b < attn_steps) & (b % len(halves) == half))
        def _(tiles=tiles):
            _attn_tiles(qa_ref, ka_ref, va_ref, km_ref, vm_ref, oa_ref, tiles, tq)

    @pl.when(b == nb - 1)
    def _():
        for gi in range(nslot):
            wait(nb - 1, gi)


def _decode_attend(page_table, qcat, c_own, k_own, cache_lat, cache_krt,
                   q_x, k_x, v_x, k_meta, v_meta, batch, seq):
    db, n_pages = page_table.shape
    gp, nslot, tq = DECODE_GROUP_PAGES, DECODE_SLOTS, TQ_ATTN
    ng = n_pages // gp
    assert n_pages % gp == 0 and ng % nslot == 0 and seq % tq == 0
    n_units = batch * N_HEADS
    halves = _balanced_halves(seq // tq)
    attn_steps = len(halves) * n_units
    assert attn_steps <= db

    def unit(b):
        u = jnp.minimum(b // len(halves), n_units - 1)
        return u % N_HEADS, u // N_HEADS

    per_seq = lambda shape: pl.BlockSpec(shape, lambda b, pt: (b, 0, 0))
    a_seq = lambda w: pl.BlockSpec((1, seq, w), lambda b, pt: (*unit(b), 0))
    a_meta = lambda w: pl.BlockSpec((1, N_META, w), lambda b, pt: (unit(b)[0], 0, 0))
    grid_spec = pltpu.PrefetchScalarGridSpec(
        num_scalar_prefetch=1,
        grid=(db,),
        in_specs=[per_seq((1, N_HEADS, QCAT)), per_seq((1, 1, KV_RANK)), per_seq((1, 1, LANES)),
                  pl.BlockSpec(memory_space=pl.ANY), pl.BlockSpec(memory_space=pl.ANY),
                  a_seq(HEAD_PAD), a_seq(HEAD_PAD), a_seq(V_DIM), a_meta(HEAD_PAD), a_meta(V_DIM)],
        out_specs=[per_seq((1, N_HEADS, KV_RANK)),
                   pl.BlockSpec((seq, V_DIM), lambda b, pt: unit(b)[::-1])],
        scratch_shapes=[pltpu.VMEM((nslot, gp, PAGE_SIZE, KV_RANK), F32),
                        pltpu.VMEM((nslot, gp, QK_ROPE, PAGE_SIZE), F32),
                        pltpu.SemaphoreType.DMA((nslot,))],
    )
    return pl.pallas_call(
        partial(_decode_attend_kernel, gp=gp, ng=ng, nslot=nslot, tq=tq, halves=halves,
                attn_steps=attn_steps),
        grid_spec=grid_spec,
        out_shape=[jax.ShapeDtypeStruct((db, N_HEADS, KV_RANK), F32),
                   jax.ShapeDtypeStruct((batch * seq, MLA_WIDTH), BF16)],
        compiler_params=pltpu.CompilerParams(
            dimension_semantics=("arbitrary",), vmem_limit_bytes=VMEM_LIMIT),
        name="paged_decode_prompt_attn",
    )(page_table, qcat, c_own, k_own, cache_lat, cache_krt, q_x, k_x, v_x, k_meta, v_meta)


def _unabsorb_kernel(lat_ref, wv_ref, o_ref):
    o_ref[...] = jnp.dot(lat_ref[...].astype(BF16), wv_ref[...], preferred_element_type=F32)


def _unabsorb(lat, wv):
    db = lat.shape[0]
    return pl.pallas_call(
        _unabsorb_kernel,
        grid=(N_HEADS,),
        in_specs=[pl.BlockSpec((db, KV_RANK), lambda h: (0, h)),
                  pl.BlockSpec((KV_RANK, V_DIM), lambda h: (0, h))],
        out_specs=pl.BlockSpec((db, V_DIM), lambda h: (0, h)),
        out_shape=jax.ShapeDtypeStruct((db, MLA_WIDTH), F32),
        compiler_params=pltpu.CompilerParams(dimension_semantics=("arbitrary",)),
        name="unabsorb_v",
    )(lat, wv)


def _rot_cols(w):
    half = w.shape[-1] // 2
    return jnp.concatenate([-w[..., half:], w[..., :half]], axis=-1)


def _rope_tables(pos):
    inv = ROPE_THETA ** (-np.arange(0, QK_ROPE, 2, dtype=np.float64) / QK_ROPE)
    ang = np.asarray(pos, np.float64)[:, None] * inv[None, :]
    ang = np.concatenate([ang, ang], axis=-1)
    z = np.zeros_like(ang)
    return (jnp.asarray(np.concatenate([np.cos(ang), z], axis=-1), F32),
            jnp.asarray(np.concatenate([np.sin(ang), z], axis=-1), F32))


def kernel(x_prompt, x_sample, cache_kv_latent, cache_k_rope, state_pool, page_table, meta_tokens,
           g_attn, w_in, g_q, w_q_b, g_kv, w_kv_b, w_pool, pool_scale, w_o, g_mlp, w_up, w_down,
           g_final):
    assert g_attn.shape[0] == 1, "single-layer trunk"
    batch, seq, _ = x_prompt.shape
    db, s1, _ = x_sample.shape
    assert s1 == 1
    assert seq >= POOL_HIST

    kr_lo = Q_RANK + KV_RANK
    win = w_in[0]
    win_r = jnp.concatenate([win[:, :kr_lo + QK_ROPE], _rot_cols(win[:, kr_lo:kr_lo + QK_ROPE]),
                             win[:, kr_lo + QK_ROPE:]], axis=1).astype(BF16)
    wq = w_q_b[0].reshape(Q_RANK, N_HEADS, QK_NOPE + QK_ROPE)
    wq_r = jnp.concatenate([wq, _rot_cols(wq[..., QK_NOPE:])], axis=-1)
    wq_r = wq_r.reshape(Q_RANK, N_HEADS * HEAD_PAD).astype(BF16)
    wk = w_kv_b[0][..., :QK_NOPE].reshape(KV_RANK, MLA_WIDTH).astype(BF16)
    wv = w_kv_b[0][..., QK_NOPE:].reshape(KV_RANK, MLA_WIDTH).astype(BF16)
    wpool = w_pool[0].astype(BF16)
    wo = w_o[0].astype(BF16)
    wu = w_up[0].astype(BF16)
    wd = w_down[0].astype(BF16)
    ga, gq, gkv, gm = g_attn, g_q, g_kv, g_mlp
    gf = g_final[None, :]
    pscale = pool_scale

    mix_w = (ga, win_r, gq, wq_r, gkv, wk, wv)
    x = x_prompt.reshape(batch * seq, D_MODEL)
    cos_x, sin_x = _rope_tables(N_META + np.arange(seq))
    lat_x, kpe_x, pool_x, q_x, k_x, v_x = _mixer_in(x, cos_x, sin_x, *mix_w, tm=TM_MIX)
    xs = x_sample.reshape(db, D_MODEL)
    x_sm = jnp.concatenate([xs, meta_tokens], axis=0)
    cos_sm, sin_sm = _rope_tables(np.concatenate([np.full(db, PAST_LEN), np.arange(N_META)]))
    lat_sm, kpe_sm, pool_sm, q_sm, k_sm, v_sm = _mixer_in(x_sm, cos_sm, sin_sm, *mix_w,
                                                          tm=db + N_META)
    lat_s, kpe_s, pool_s = lat_sm[:db], kpe_sm[:db], pool_sm[:db]
    lat_m, kpe_m, pool_m = lat_sm[db:], kpe_sm[db:], pool_sm[db:]

    qcat = _absorb(q_sm, wk, db).reshape(db, N_HEADS, QCAT)
    n_phys = cache_kv_latent.shape[1]
    dec, o_x = _decode_attend(
        page_table, qcat, lat_s.reshape(db, 1, KV_RANK), kpe_s.reshape(db, 1, LANES),
        cache_kv_latent.reshape(n_phys, PAGE_SIZE, KV_RANK),
        jnp.swapaxes(cache_k_rope[0], 1, 2),
        q_x, k_x, v_x, k_sm[:, db:], v_sm[:, db:], batch, seq)

    h_x, hn_x = _mix_out(x, o_x, pool_x, pool_m, wpool, pscale, wo, gm, tm=TM_OUT, seq=seq)
    y_x = _mlp(h_x, hn_x, wu, wd, gf, tm=TM_MLP, tf=TF_MLP)

    per_seq = lambda a: a.reshape(batch, seq, a.shape[-1])
    with_meta = lambda m, a: jnp.concatenate(
        [jnp.broadcast_to(m[None], (batch,) + m.shape), per_seq(a)], axis=1)[None]
    y_prompt = per_seq(y_x)
    new_lat_prompt = with_meta(lat_m, lat_x)
    new_krope_prompt = with_meta(kpe_m[:, :QK_ROPE], kpe_x[:, :QK_ROPE])
    new_pool_prompt = per_seq(pool_x)[None, :, seq - POOL_HIST:]

    a_s = _unabsorb(dec.reshape(db, N_HEADS * KV_RANK), wv)
    state = state_pool[0]
    h_s, hn_s = _mix_out_sample(xs, a_s, pool_s, state.reshape(db, POOL_HIST * POOL_WIDTH),
                                wpool, pscale, wo, gm)
    y_s = _mlp(h_s, hn_s, wu, wd, gf, tm=db, tf=TF_MLP)

    y_sample = y_s.reshape(db, 1, D_MODEL)
    new_lat_sample = lat_s.reshape(1, db, 1, KV_RANK)
    new_krope_sample = kpe_s[:, :QK_ROPE].reshape(1, db, 1, QK_ROPE)
    new_pool_sample = jnp.concatenate([state[:, 1:], pool_s[:, None, :]], axis=1)[None]
    return (y_prompt, y_sample, new_lat_prompt, new_krope_prompt, new_pool_prompt,
            new_lat_sample, new_krope_sample, new_pool_sample)
```

```python
from functools import partial

import jax
import jax.numpy as jnp
import numpy as np
from jax import lax
from jax.experimental import pallas as pl
from jax.experimental.pallas import tpu as pltpu

D_MODEL = 2048
N_META = 16
N_HEADS = 12
QK_NOPE = 128
QK_ROPE = 64
V_DIM = 128
Q_RANK = 512
KV_RANK = 512
MLA_WIDTH = N_HEADS * V_DIM
POOL_WIDTH = D_MODEL - MLA_WIDTH
POOL_WINDOWS = (2, 4, 8, 16)
POOL_GROUP = 128
POOL_HIST = 15
D_FF = 4 * D_MODEL
ROPE_THETA = 10000.0
NORM_EPS = 1e-6
PAST_LEN = 16384
PAGE_SIZE = 128

LANES = 128
HEAD_PAD = 256
QCAT = 640
IN_COLS_R = Q_RANK + KV_RANK + 2 * QK_ROPE + POOL_WIDTH
SCALE = (QK_NOPE + QK_ROPE) ** -0.5
VMEM_LIMIT = 56 * 1024 * 1024
TM_MIX = 256
TM_OUT = 512
TQ_ATTN = 256
TM_MLP = 512
TF_MLP = 1024
DECODE_GROUP_PAGES = 16
DECODE_SLOTS = 8
DECODE_WAIT_GROUPS = 4

BF16 = jnp.bfloat16
F32 = jnp.float32


def _rms(x, g):
    return x * lax.rsqrt(jnp.mean(x * x, axis=-1, keepdims=True) + NORM_EPS) * g


def _rope128(x, c, s):
    return x * c + pltpu.roll(x, QK_ROPE, axis=1) * s


def _const_spec(shape):
    n = len(shape)
    return pl.BlockSpec(shape, lambda *_: (0,) * n, pipeline_mode=pl.Buffered(1))


def _mixer_in_kernel(x_ref, ga_ref, win_ref, gq_ref, wq_ref, gkv_ref, wk_ref, wv_ref,
                     c_ref, s_ref, *rest, n_cast):
    cast_in, rest = rest[:n_cast], rest[n_cast:]
    lat_ref, kpe_ref, pool_ref, q_ref, k_ref, v_ref = rest[:6]
    for w32_ref, w16_ref in zip(cast_in, rest[6:]):
        w16_ref[...] = w32_ref[...].astype(BF16)

    h = _rms(x_ref[...], ga_ref[...]).astype(BF16)
    proj = jnp.dot(h, win_ref[...], preferred_element_type=F32)
    cos = c_ref[...]
    sin = s_ref[...]
    ckv = _rms(proj[:, Q_RANK:Q_RANK + KV_RANK], gkv_ref[...])
    lat_ref[...] = ckv
    kpe = _rope128(proj[:, 2 * Q_RANK:2 * Q_RANK + LANES], cos, sin)
    kpe_ref[...] = kpe
    pool_ref[...] = proj[:, 2 * Q_RANK + LANES:]
    kpe_b = kpe.astype(BF16)

    qn = _rms(proj[:, :Q_RANK], gq_ref[...]).astype(BF16)
    q = jnp.dot(qn, wq_ref[...], preferred_element_type=F32)
    for hh in range(N_HEADS):
        base = hh * HEAD_PAD
        q_ref[hh, :, :LANES] = (q[:, base:base + LANES] * SCALE).astype(BF16)
        pe = _rope128(q[:, base + LANES:base + HEAD_PAD], cos, sin) * SCALE
        q_ref[hh, :, LANES:] = pe.astype(BF16)

    cb = ckv.astype(BF16)
    kn = jnp.dot(cb, wk_ref[...], preferred_element_type=F32)
    vv = jnp.dot(cb, wv_ref[...], preferred_element_type=F32)
    for hh in range(N_HEADS):
        k_ref[hh, :, :LANES] = kn[:, hh * LANES:(hh + 1) * LANES].astype(BF16)
        k_ref[hh, :, LANES:] = kpe_b
        v_ref[hh] = vv[:, hh * LANES:(hh + 1) * LANES].astype(BF16)


def _mixer_in(x, cos, sin, ga, win, gq, wq, gkv, wk, wv, tm, cast=()):
    t = x.shape[0]
    steps = t // tm
    table_blocks = cos.shape[0] // tm
    assert t % tm == 0 and cos.shape[0] % tm == 0
    assert all(w.shape[0] % steps == 0 for w in cast)
    row = lambda w: pl.BlockSpec((tm, w), lambda i: (i, 0))
    table = pl.BlockSpec((tm, LANES), lambda i: (i % table_blocks, 0))
    head = lambda w: pl.BlockSpec((N_HEADS, tm, w), lambda i: (0, i, 0))
    cast_specs = [pl.BlockSpec((w.shape[0] // steps, w.shape[1]), lambda i: (i, 0)) for w in cast]
    return pl.pallas_call(
        partial(_mixer_in_kernel, n_cast=len(cast)),
        grid=(steps,),
        in_specs=[row(D_MODEL), _const_spec((1, D_MODEL)), _const_spec((D_MODEL, IN_COLS_R)),
                  _const_spec((1, Q_RANK)), _const_spec((Q_RANK, N_HEADS * HEAD_PAD)),
                  _const_spec((1, KV_RANK)), _const_spec((KV_RANK, MLA_WIDTH)),
                  _const_spec((KV_RANK, MLA_WIDTH)), table, table] + cast_specs,
        out_specs=[row(KV_RANK), row(LANES), row(POOL_WIDTH),
                   head(HEAD_PAD), head(HEAD_PAD), head(V_DIM)] + cast_specs,
        out_shape=[jax.ShapeDtypeStruct((t, KV_RANK), F32),
                   jax.ShapeDtypeStruct((t, LANES), F32),
                   jax.ShapeDtypeStruct((t, POOL_WIDTH), F32),
                   jax.ShapeDtypeStruct((N_HEADS, t, HEAD_PAD), BF16),
                   jax.ShapeDtypeStruct((N_HEADS, t, HEAD_PAD), BF16),
                   jax.ShapeDtypeStruct((N_HEADS, t, V_DIM), BF16)]
                  + [jax.ShapeDtypeStruct(w.shape, BF16) for w in cast],
        compiler_params=pltpu.CompilerParams(
            dimension_semantics=("arbitrary",), vmem_limit_bytes=VMEM_LIMIT),
        name="mixer_in",
    )(x, ga, win, gq, wq, gkv, wk, wv, cos, sin, *cast)


_NT = (((1,), (1,)), ((), ()))


def _attn_tiles(q_ref, k_ref, v_ref, km_ref, vm_ref, o_ref, tiles, tq):
    km = km_ref[0]
    vm = vm_ref[0]
    row = lax.broadcasted_iota(jnp.int32, (tq, tq), 0)
    col = lax.broadcasted_iota(jnp.int32, (tq, tq), 1)
    for ti in tiles:
        r0 = ti * tq
        q = q_ref[0, r0:r0 + tq, :]
        s_meta = lax.dot_general(q, km, _NT, preferred_element_type=F32)
        s_diag = lax.dot_general(q, k_ref[0, r0:r0 + tq, :], _NT, preferred_element_type=F32)
        s_diag = jnp.where(col <= row, s_diag, -jnp.inf)
        m = jnp.maximum(jnp.max(s_diag, axis=-1, keepdims=True),
                        jnp.max(s_meta, axis=-1, keepdims=True))
        if r0:
            s_left = lax.dot_general(q, k_ref[0, :r0, :], _NT, preferred_element_type=F32)
            m = jnp.maximum(m, jnp.max(s_left, axis=-1, keepdims=True))
            p_left = jnp.exp(s_left - m)
        p_diag = jnp.exp(s_diag - m)
        p_meta = jnp.exp(s_meta - m)
        l = jnp.sum(p_diag, axis=-1, keepdims=True) + jnp.sum(p_meta, axis=-1, keepdims=True)
        o = (jnp.dot(p_diag.astype(BF16), v_ref[0, r0:r0 + tq, :], preferred_element_type=F32)
             + jnp.dot(p_meta.astype(BF16), vm, preferred_element_type=F32))
        if r0:
            l = l + jnp.sum(p_left, axis=-1, keepdims=True)
            o = o + jnp.dot(p_left.astype(BF16), v_ref[0, :r0, :], preferred_element_type=F32)
        o_ref[r0:r0 + tq, :] = (o / l).astype(BF16)


def _balanced_halves(n_tiles):
    assert n_tiles % 4 == 0
    pairs = [(i, n_tiles - 1 - i) for i in range(n_tiles // 2)]
    return (tuple(t for p in pairs[0::2] for t in p), tuple(t for p in pairs[1::2] for t in p))


def _mix_project(x, attn_o, mixed_fn, wpool_ref, pscale_ref, wo_ref, gm_ref, h_ref, hn_ref):
    parts = [attn_o]
    for g in range(len(POOL_WINDOWS)):
        lo = g * POOL_GROUP
        y = jnp.dot(mixed_fn(g).astype(BF16), wpool_ref[g], preferred_element_type=F32)
        parts.append((y * pscale_ref[:, lo:lo + POOL_GROUP]).astype(BF16))
    h = x + jnp.dot(jnp.concatenate(parts, axis=1), wo_ref[...], preferred_element_type=F32)
    h_ref[...] = h
    hn_ref[...] = _rms(h, gm_ref[...]).astype(BF16)


def _mix_out_kernel(x_ref, o_ref, pool_ref, halo_ref, pmeta_ref, wpool_ref, pscale_ref, wo_ref,
                    gm_ref, h_ref, hn_ref, ext_ref, *, tm, tiles_per_seq):
    i = pl.program_id(0)
    first = (i % tiles_per_seq) == 0
    ext_ref[0:N_META, :] = jnp.where(first, pmeta_ref[...], halo_ref[...])
    ext_ref[N_META:, :] = pool_ref[...]

    def mixed(g):
        w = POOL_WINDOWS[g]
        lo = g * POOL_GROUP
        cur = ext_ref[N_META:N_META + tm, lo:lo + POOL_GROUP]
        tot = cur
        for kk in range(1, w):
            tot = tot + ext_ref[N_META - kk:N_META - kk + tm, lo:lo + POOL_GROUP]
        return tot / float(w) - cur

    _mix_project(x_ref[...], o_ref[...], mixed, wpool_ref, pscale_ref, wo_ref, gm_ref,
                 h_ref, hn_ref)


def _mix_out(x, o, pool, pool_meta, wpool, pscale, wo, gm, tm, seq):
    t = x.shape[0]
    assert seq % tm == 0 and tm % N_META == 0 and N_META > POOL_HIST
    row = lambda w: pl.BlockSpec((tm, w), lambda i: (i, 0))
    hb = tm // N_META
    return pl.pallas_call(
        partial(_mix_out_kernel, tm=tm, tiles_per_seq=seq // tm),
        grid=(t // tm,),
        in_specs=[row(D_MODEL), row(MLA_WIDTH), row(POOL_WIDTH),
                  pl.BlockSpec((N_META, POOL_WIDTH), lambda i: (jnp.maximum(i * hb - 1, 0), 0)),
                  _const_spec((N_META, POOL_WIDTH)),
                  _const_spec((len(POOL_WINDOWS), POOL_GROUP, POOL_GROUP)),
                  _const_spec((1, POOL_WIDTH)), _const_spec((D_MODEL, D_MODEL)),
                  _const_spec((1, D_MODEL))],
        out_specs=[row(D_MODEL), row(D_MODEL)],
        out_shape=[jax.ShapeDtypeStruct((t, D_MODEL), F32),
                   jax.ShapeDtypeStruct((t, D_MODEL), BF16)],
        scratch_shapes=[pltpu.VMEM((tm + N_META, POOL_WIDTH), F32)],
        compiler_params=pltpu.CompilerParams(
            dimension_semantics=("arbitrary",), vmem_limit_bytes=VMEM_LIMIT),
        name="mix_out",
    )(x, o, pool, pool, pool_meta, wpool, pscale, wo, gm)


def _mix_out_sample_kernel(x_ref, o_ref, pin_ref, st_ref, wpool_ref, pscale_ref, wo_ref, gm_ref,
                           h_ref, hn_ref):
    def mixed(g):
        w = POOL_WINDOWS[g]
        lo = g * POOL_GROUP
        cur = pin_ref[:, lo:lo + POOL_GROUP]
        tot = cur
        for kk in range(1, w):
            base = (POOL_HIST - kk) * POOL_WIDTH + lo
            tot = tot + st_ref[:, base:base + POOL_GROUP]
        return tot / float(w) - cur

    _mix_project(x_ref[...], o_ref[...].astype(BF16), mixed, wpool_ref, pscale_ref, wo_ref,
                 gm_ref, h_ref, hn_ref)


def _mix_out_sample(x, o, pin, state, wpool, pscale, wo, gm):
    t = x.shape[0]
    full = lambda a: pl.BlockSpec(a.shape, lambda: (0,) * a.ndim)
    args = (x, o, pin, state, wpool, pscale, wo, gm)
    return pl.pallas_call(
        _mix_out_sample_kernel,
        in_specs=[full(a) for a in args],
        out_specs=[pl.BlockSpec((t, D_MODEL), lambda: (0, 0))] * 2,
        out_shape=[jax.ShapeDtypeStruct((t, D_MODEL), F32),
                   jax.ShapeDtypeStruct((t, D_MODEL), BF16)],
        compiler_params=pltpu.CompilerParams(vmem_limit_bytes=VMEM_LIMIT),
        name="mix_out_sample",
    )(*args)


def _mlp_kernel(h_ref, hn_ref, wu_ref, wd_ref, gf_ref, y_ref):
    f = pl.program_id(1)

    @pl.when(f == 0)
    def _():
        y_ref[...] = h_ref[...]

    z = jnp.dot(hn_ref[...], wu_ref[...], preferred_element_type=F32)
    a = jnp.square(jnp.maximum(z, 0.0)).astype(BF16)
    y_ref[...] += jnp.dot(a, wd_ref[...], preferred_element_type=F32)

    @pl.when(f == pl.num_programs(1) - 1)
    def _():
        y_ref[...] = _rms(y_ref[...], gf_ref[...])


def _mlp(h, hn, wu, wd, gf, tm, tf):
    t = h.shape[0]
    assert t % tm == 0 and D_FF % tf == 0
    return pl.pallas_call(
        _mlp_kernel,
        grid=(t // tm, D_FF // tf),
        in_specs=[pl.BlockSpec((tm, D_MODEL), lambda i, f: (i, 0)),
                  pl.BlockSpec((tm, D_MODEL), lambda i, f: (i, 0)),
                  pl.BlockSpec((D_MODEL, tf), lambda i, f: (0, f)),
                  pl.BlockSpec((tf, D_MODEL), lambda i, f: (f, 0)),
                  pl.BlockSpec((1, D_MODEL), lambda i, f: (0, 0))],
        out_specs=pl.BlockSpec((tm, D_MODEL), lambda i, f: (i, 0)),
        out_shape=jax.ShapeDtypeStruct((t, D_MODEL), F32),
        compiler_params=pltpu.CompilerParams(
            dimension_semantics=("arbitrary", "arbitrary"), vmem_limit_bytes=VMEM_LIMIT),
        name="mlp",
    )(h, hn, wu, wd, gf)


def _absorb_kernel(q_ref, wk_ref, o_ref):
    q = q_ref[0]
    qlat = lax.dot_general(q[:, :LANES], wk_ref[...], _NT,
                           preferred_element_type=F32)
    o_ref[:, :KV_RANK] = qlat
    o_ref[:, KV_RANK:] = q[:, LANES:].astype(F32)


def _absorb(q, wk, db):
    return pl.pallas_call(
        _absorb_kernel,
        grid=(N_HEADS,),
        in_specs=[pl.BlockSpec((1, db, HEAD_PAD), lambda h: (h, 0, 0)),
                  pl.BlockSpec((KV_RANK, LANES), lambda h: (0, h))],
        out_specs=pl.BlockSpec((db, QCAT), lambda h: (0, h)),
        out_shape=jax.ShapeDtypeStruct((db, N_HEADS * QCAT), F32),
        compiler_params=pltpu.CompilerParams(dimension_semantics=("arbitrary",)),
        name="absorb_q",
    )(q, wk)


def _decode_attend_kernel(pt_ref, q_ref, cown_ref, kown_ref, lat_hbm, krt_hbm,
                          qa_ref, ka_ref, va_ref, km_ref, vm_ref, o_ref, oa_ref,
                          latbuf, krtbuf, sem, *, gp, ng, nslot, wchunk, tq, halves, attn_steps):
    b = pl.program_id(0)
    nb = pl.num_programs(0)

    def page_copies(bb, gi, k):
        slot = gi % nslot
        page = pt_ref[bb, gi * gp + k]
        return (pltpu.make_async_copy(lat_hbm.at[page], latbuf.at[slot, k], sem.at[slot]),
                pltpu.make_async_copy(krt_hbm.at[page], krtbuf.at[slot, k], sem.at[slot]))

    def start(bb, gi):
        for k in range(gp):
            for cp in page_copies(bb, gi, k):
                cp.start()

    def wait(bb, gi):
        for k in range(gp):
            for cp in page_copies(bb, gi, k):
                cp.wait()

    def refill_target(gi):
        nxt = gi + nslot
        if nxt < ng:
            return b, nxt
        return jnp.minimum(b + 1, nb - 1), nxt - ng

    @pl.when(b == 0)
    def _():
        for gi in range(nslot):
            start(0, gi)

    q = q_ref[0]
    qlat = q[:, :KV_RANK]
    qpe = q[:, KV_RANK:KV_RANK + QK_ROPE]

    def scores(gi):
        slot = gi % nslot
        parts = [lax.dot_general(qlat, latbuf[slot, k], _NT, preferred_element_type=F32)
                 + jnp.dot(qpe, krtbuf[slot, k], preferred_element_type=F32)
                 for k in range(gp)]
        return jnp.concatenate(parts, axis=-1)

    def fold(gi, s, m, l, acc):
        slot = gi % nslot
        m_new = jnp.maximum(m, jnp.max(s, axis=-1, keepdims=True))
        corr = jnp.exp(m - m_new)
        p = jnp.exp(s - m_new)
        l = l * corr + jnp.sum(p, axis=-1, keepdims=True)
        acc = acc * corr
        for k in range(gp):
            acc = acc + jnp.dot(p[:, k * PAGE_SIZE:(k + 1) * PAGE_SIZE], latbuf[slot, k],
                                preferred_element_type=F32)
        start(*refill_target(gi))
        return m_new, l, acc

    c_own = cown_ref[0]
    k_own = kown_ref[0]
    m = (jnp.sum(qlat * c_own, axis=-1, keepdims=True)
         + jnp.sum(qpe * k_own[:, :QK_ROPE], axis=-1, keepdims=True))
    l = jnp.ones_like(m)
    acc = jnp.broadcast_to(c_own, (N_HEADS, KV_RANK))

    for gw in range(wchunk):
        wait(b, gw)
    s_cur = scores(0)
    for gi in range(ng):
        if gi + 1 < ng:
            if (gi + 1) % wchunk == 0:
                for gw in range(gi + 1, gi + 1 + wchunk):
                    wait(b, gw)
            s_next = scores(gi + 1)
        m, l, acc = fold(gi, s_cur, m, l, acc)
        if gi + 1 < ng:
            s_cur = s_next
    o_ref[0] = acc / l

    for half, tiles in enumerate(halves):
        @pl.when((b < attn_steps) & (b % len(halves) == half))
        def _(tiles=tiles):
            _attn_tiles(qa_ref, ka_ref, va_ref, km_ref, vm_ref, oa_ref, tiles, tq)

    @pl.when(b == nb - 1)
    def _():
        for gi in range(nslot):
            wait(nb - 1, gi)


def _decode_attend(page_table, qcat, c_own, k_own, cache_lat, cache_krt,
                   q_x, k_x, v_x, k_meta, v_meta, batch, seq):
    db, n_pages = page_table.shape
    gp, nslot, wchunk, tq = DECODE_GROUP_PAGES, DECODE_SLOTS, DECODE_WAIT_GROUPS, TQ_ATTN
    ng = n_pages // gp
    assert n_pages % gp == 0 and ng % nslot == 0 and seq % tq == 0
    assert ng % wchunk == 0 and wchunk < nslot
    n_units = batch * N_HEADS
    halves = _balanced_halves(seq // tq)
    attn_steps = len(halves) * n_units
    assert attn_steps <= db

    def unit(b):
        u = jnp.minimum(b // len(halves), n_units - 1)
        return u % N_HEADS, u // N_HEADS

    per_seq = lambda shape: pl.BlockSpec(shape, lambda b, pt: (b, 0, 0))
    a_seq = lambda w: pl.BlockSpec((1, seq, w), lambda b, pt: (*unit(b), 0))
    a_meta = lambda w: pl.BlockSpec((1, N_META, w), lambda b, pt: (unit(b)[0], 0, 0))
    grid_spec = pltpu.PrefetchScalarGridSpec(
        num_scalar_prefetch=1,
        grid=(db,),
        in_specs=[per_seq((1, N_HEADS, QCAT)), per_seq((1, 1, KV_RANK)), per_seq((1, 1, LANES)),
                  pl.BlockSpec(memory_space=pl.ANY), pl.BlockSpec(memory_space=pl.ANY),
                  a_seq(HEAD_PAD), a_seq(HEAD_PAD), a_seq(V_DIM), a_meta(HEAD_PAD), a_meta(V_DIM)],
        out_specs=[per_seq((1, N_HEADS, KV_RANK)),
                   pl.BlockSpec((seq, V_DIM), lambda b, pt: unit(b)[::-1])],
        scratch_shapes=[pltpu.VMEM((nslot, gp, PAGE_SIZE, KV_RANK), F32),
                        pltpu.VMEM((nslot, gp, QK_ROPE, PAGE_SIZE), F32),
                        pltpu.SemaphoreType.DMA((nslot,))],
    )
    return pl.pallas_call(
        partial(_decode_attend_kernel, gp=gp, ng=ng, nslot=nslot, wchunk=wchunk, tq=tq, halves=halves,
                attn_steps=attn_steps),
        grid_spec=grid_spec,
        out_shape=[jax.ShapeDtypeStruct((db, N_HEADS, KV_RANK), F32),
                   jax.ShapeDtypeStruct((batch * seq, MLA_WIDTH), BF16)],
        compiler_params=pltpu.CompilerParams(
            dimension_semantics=("arbitrary",), vmem_limit_bytes=VMEM_LIMIT),
        name="paged_decode_prompt_attn",
    )(page_table, qcat, c_own, k_own, cache_lat, cache_krt, q_x, k_x, v_x, k_meta, v_meta)


def _unabsorb_kernel(lat_ref, wv_ref, o_ref):
    o_ref[...] = jnp.dot(lat_ref[...].astype(BF16), wv_ref[...], preferred_element_type=F32)


def _unabsorb(lat, wv):
    db = lat.shape[0]
    return pl.pallas_call(
        _unabsorb_kernel,
        grid=(N_HEADS,),
        in_specs=[pl.BlockSpec((db, KV_RANK), lambda h: (0, h)),
                  pl.BlockSpec((KV_RANK, V_DIM), lambda h: (0, h))],
        out_specs=pl.BlockSpec((db, V_DIM), lambda h: (0, h)),
        out_shape=jax.ShapeDtypeStruct((db, MLA_WIDTH), F32),
        compiler_params=pltpu.CompilerParams(dimension_semantics=("arbitrary",)),
        name="unabsorb_v",
    )(lat, wv)


def _rot_cols(w):
    half = w.shape[-1] // 2
    return jnp.concatenate([-w[..., half:], w[..., :half]], axis=-1)


def _rope_tables(pos):
    inv = ROPE_THETA ** (-np.arange(0, QK_ROPE, 2, dtype=np.float64) / QK_ROPE)
    ang = np.asarray(pos, np.float64)[:, None] * inv[None, :]
    ang = np.concatenate([ang, ang], axis=-1)
    z = np.zeros_like(ang)
    return (jnp.asarray(np.concatenate([np.cos(ang), z], axis=-1), F32),
            jnp.asarray(np.concatenate([np.sin(ang), z], axis=-1), F32))


def kernel(x_prompt, x_sample, cache_kv_latent, cache_k_rope, state_pool, page_table, meta_tokens,
           g_attn, w_in, g_q, w_q_b, g_kv, w_kv_b, w_pool, pool_scale, w_o, g_mlp, w_up, w_down,
           g_final):
    assert g_attn.shape[0] == 1, "single-layer trunk"
    batch, seq, _ = x_prompt.shape
    db, s1, _ = x_sample.shape
    assert s1 == 1
    assert seq >= POOL_HIST

    kr_lo = Q_RANK + KV_RANK
    win = w_in[0]
    win_r = jnp.concatenate([win[:, :kr_lo + QK_ROPE], _rot_cols(win[:, kr_lo:kr_lo + QK_ROPE]),
                             win[:, kr_lo + QK_ROPE:]], axis=1).astype(BF16)
    wq = w_q_b[0].reshape(Q_RANK, N_HEADS, QK_NOPE + QK_ROPE)
    wq_r = jnp.concatenate([wq, _rot_cols(wq[..., QK_NOPE:])], axis=-1)
    wq_r = wq_r.reshape(Q_RANK, N_HEADS * HEAD_PAD).astype(BF16)
    wk = w_kv_b[0][..., :QK_NOPE].reshape(KV_RANK, MLA_WIDTH).astype(BF16)
    wv = w_kv_b[0][..., QK_NOPE:].reshape(KV_RANK, MLA_WIDTH).astype(BF16)
    wpool = w_pool[0].astype(BF16)
    wd = w_down[0].astype(BF16)
    ga, gq, gkv, gm = g_attn, g_q, g_kv, g_mlp
    gf = g_final[None, :]
    pscale = pool_scale

    mix_w = (ga, win_r, gq, wq_r, gkv, wk, wv)
    x = x_prompt.reshape(batch * seq, D_MODEL)
    cos_x, sin_x = _rope_tables(N_META + np.arange(seq))
    lat_x, kpe_x, pool_x, q_x, k_x, v_x, wo, wu = _mixer_in(
        x, cos_x, sin_x, *mix_w, tm=TM_MIX, cast=(w_o[0], w_up[0]))
    xs = x_sample.reshape(db, D_MODEL)
    x_sm = jnp.concatenate([xs, meta_tokens], axis=0)
    cos_sm, sin_sm = _rope_tables(np.concatenate([np.full(db, PAST_LEN), np.arange(N_META)]))
    lat_sm, kpe_sm, pool_sm, q_sm, k_sm, v_sm = _mixer_in(x_sm, cos_sm, sin_sm, *mix_w,
                                                          tm=db + N_META)
    lat_s, kpe_s, pool_s = lat_sm[:db], kpe_sm[:db], pool_sm[:db]
    lat_m, kpe_m, pool_m = lat_sm[db:], kpe_sm[db:], pool_sm[db:]

    qcat = _absorb(q_sm, wk, db).reshape(db, N_HEADS, QCAT)
    n_phys = cache_kv_latent.shape[1]
    dec, o_x = _decode_attend(
        page_table, qcat, lat_s.reshape(db, 1, KV_RANK), kpe_s.reshape(db, 1, LANES),
        cache_kv_latent.reshape(n_phys, PAGE_SIZE, KV_RANK),
        jnp.swapaxes(cache_k_rope[0], 1, 2),
        q_x, k_x, v_x, k_sm[:, db:], v_sm[:, db:], batch, seq)

    h_x, hn_x = _mix_out(x, o_x, pool_x, pool_m, wpool, pscale, wo, gm, tm=TM_OUT, seq=seq)
    y_x = _mlp(h_x, hn_x, wu, wd, gf, tm=TM_MLP, tf=TF_MLP)

    per_seq = lambda a: a.reshape(batch, seq, a.shape[-1])
    with_meta = lambda m, a: jnp.concatenate(
        [jnp.broadcast_to(m[None], (batch,) + m.shape), per_seq(a)], axis=1)[None]
    y_prompt = per_seq(y_x)
    new_lat_prompt = with_meta(lat_m, lat_x)
    new_krope_prompt = with_meta(kpe_m[:, :QK_ROPE], kpe_x[:, :QK_ROPE])
    new_pool_prompt = per_seq(pool_x)[None, :, seq - POOL_HIST:]

    a_s = _unabsorb(dec.reshape(db, N_HEADS * KV_RANK), wv)
    state = state_pool[0]
    h_s, hn_s = _mix_out_sample(xs, a_s, pool_s, state.reshape(db, POOL_HIST * POOL_WIDTH),
                                wpool, pscale, wo, gm)
    y_s = _mlp(h_s, hn_s, wu, wd, gf, tm=db, tf=TF_MLP)

    y_sample = y_s.reshape(db, 1, D_MODEL)
    new_lat_sample = lat_s.reshape(1, db, 1, KV_RANK)
    new_krope_sample = kpe_s[:, :QK_ROPE].reshape(1, db, 1, QK_ROPE)
    new_pool_sample = jnp.concatenate([state[:, 1:], pool_s[:, None, :]], axis=1)[None]
    return (y_prompt, y_sample, new_lat_prompt, new_krope_prompt, new_pool_prompt,
            new_lat_sample, new_krope_sample, new_pool_sample)
```

```python
from functools import partial

import jax
import jax.numpy as jnp
import numpy as np
from jax import lax
from jax.experimental import pallas as pl
from jax.experimental.pallas import tpu as pltpu

D_MODEL = 2048
N_META = 16
N_HEADS = 12
QK_NOPE = 128
QK_ROPE = 64
V_DIM = 128
Q_RANK = 512
KV_RANK = 512
MLA_WIDTH = N_HEADS * V_DIM
POOL_WIDTH = D_MODEL - MLA_WIDTH
POOL_WINDOWS = (2, 4, 8, 16)
POOL_GROUP = 128
POOL_HIST = 15
D_FF = 4 * D_MODEL
ROPE_THETA = 10000.0
NORM_EPS = 1e-6
PAST_LEN = 16384
PAGE_SIZE = 128

LANES = 128
HEAD_PAD = 256
QCAT = 640
IN_COLS_R = Q_RANK + KV_RANK + 2 * QK_ROPE + POOL_WIDTH
SCALE = (QK_NOPE + QK_ROPE) ** -0.5
VMEM_LIMIT = 56 * 1024 * 1024
TM_MIX = 256
TM_OUT = 512
TQ_ATTN = 256
TM_MLP = 512
TF_MLP = 1024
DECODE_GROUP_PAGES = 16
DECODE_SLOTS = 8
DECODE_WAIT_GROUPS = 4

BF16 = jnp.bfloat16
F32 = jnp.float32


def _rms(x, g):
    return x * lax.rsqrt(jnp.mean(x * x, axis=-1, keepdims=True) + NORM_EPS) * g


def _rope128(x, c, s):
    return x * c + pltpu.roll(x, QK_ROPE, axis=1) * s


def _const_spec(shape):
    n = len(shape)
    return pl.BlockSpec(shape, lambda *_: (0,) * n, pipeline_mode=pl.Buffered(1))


def _mixer_in_kernel(x_ref, ga_ref, win_ref, gq_ref, wq_ref, gkv_ref, wk_ref, wv_ref,
                     c_ref, s_ref, *rest, n_cast):
    cast_in, rest = rest[:n_cast], rest[n_cast:]
    lat_ref, kpe_ref, pool_ref, q_ref, k_ref, v_ref = rest[:6]
    for w32_ref, w16_ref in zip(cast_in, rest[6:]):
        w16_ref[...] = w32_ref[...].astype(BF16)

    h = _rms(x_ref[...], ga_ref[...]).astype(BF16)
    proj = jnp.dot(h, win_ref[...], preferred_element_type=F32)
    cos = c_ref[...]
    sin = s_ref[...]
    ckv = _rms(proj[:, Q_RANK:Q_RANK + KV_RANK], gkv_ref[...])
    lat_ref[...] = ckv
    kpe = _rope128(proj[:, 2 * Q_RANK:2 * Q_RANK + LANES], cos, sin)
    kpe_ref[...] = kpe
    pool_ref[...] = proj[:, 2 * Q_RANK + LANES:]
    kpe_b = kpe.astype(BF16)

    qn = _rms(proj[:, :Q_RANK], gq_ref[...]).astype(BF16)
    q = jnp.dot(qn, wq_ref[...], preferred_element_type=F32)
    for hh in range(N_HEADS):
        base = hh * HEAD_PAD
        q_ref[hh, :, :LANES] = (q[:, base:base + LANES] * SCALE).astype(BF16)
        pe = _rope128(q[:, base + LANES:base + HEAD_PAD], cos, sin) * SCALE
        q_ref[hh, :, LANES:] = pe.astype(BF16)

    cb = ckv.astype(BF16)
    kn = jnp.dot(cb, wk_ref[...], preferred_element_type=F32)
    vv = jnp.dot(cb, wv_ref[...], preferred_element_type=F32)
    for hh in range(N_HEADS):
        k_ref[hh, :, :LANES] = kn[:, hh * LANES:(hh + 1) * LANES].astype(BF16)
        k_ref[hh, :, LANES:] = kpe_b
        v_ref[hh] = vv[:, hh * LANES:(hh + 1) * LANES].astype(BF16)


def _mixer_in(x, cos, sin, ga, win, gq, wq, gkv, wk, wv, tm, cast=()):
    t = x.shape[0]
    steps = t // tm
    table_blocks = cos.shape[0] // tm
    assert t % tm == 0 and cos.shape[0] % tm == 0
    assert all(w.shape[0] % steps == 0 for w in cast)
    row = lambda w: pl.BlockSpec((tm, w), lambda i: (i, 0))
    table = pl.BlockSpec((tm, LANES), lambda i: (i % table_blocks, 0))
    head = lambda w: pl.BlockSpec((N_HEADS, tm, w), lambda i: (0, i, 0))
    cast_specs = [pl.BlockSpec((w.shape[0] // steps, w.shape[1]), lambda i: (i, 0)) for w in cast]
    return pl.pallas_call(
        partial(_mixer_in_kernel, n_cast=len(cast)),
        grid=(steps,),
        in_specs=[row(D_MODEL), _const_spec((1, D_MODEL)), _const_spec((D_MODEL, IN_COLS_R)),
                  _const_spec((1, Q_RANK)), _const_spec((Q_RANK, N_HEADS * HEAD_PAD)),
                  _const_spec((1, KV_RANK)), _const_spec((KV_RANK, MLA_WIDTH)),
                  _const_spec((KV_RANK, MLA_WIDTH)), table, table] + cast_specs,
        out_specs=[row(KV_RANK), row(LANES), row(POOL_WIDTH),
                   head(HEAD_PAD), head(HEAD_PAD), head(V_DIM)] + cast_specs,
        out_shape=[jax.ShapeDtypeStruct((t, KV_RANK), F32),
                   jax.ShapeDtypeStruct((t, LANES), F32),
                   jax.ShapeDtypeStruct((t, POOL_WIDTH), F32),
                   jax.ShapeDtypeStruct((N_HEADS, t, HEAD_PAD), BF16),
                   jax.ShapeDtypeStruct((N_HEADS, t, HEAD_PAD), BF16),
                   jax.ShapeDtypeStruct((N_HEADS, t, V_DIM), BF16)]
                  + [jax.ShapeDtypeStruct(w.shape, BF16) for w in cast],
        compiler_params=pltpu.CompilerParams(
            dimension_semantics=("arbitrary",), vmem_limit_bytes=VMEM_LIMIT),
        name="mixer_in",
    )(x, ga, win, gq, wq, gkv, wk, wv, cos, sin, *cast)


_NT = (((1,), (1,)), ((), ()))


def _attn_tiles(q_ref, k_ref, v_ref, km_ref, vm_ref, o_ref, tiles, tq):
    km = km_ref[0]
    vm = vm_ref[0]
    row = lax.broadcasted_iota(jnp.int32, (tq, tq), 0)
    col = lax.broadcasted_iota(jnp.int32, (tq, tq), 1)
    for ti in tiles:
        r0 = ti * tq
        q = q_ref[0, r0:r0 + tq, :]
        s_meta = lax.dot_general(q, km, _NT, preferred_element_type=F32)
        s_diag = lax.dot_general(q, k_ref[0, r0:r0 + tq, :], _NT, preferred_element_type=F32)
        s_diag = jnp.where(col <= row, s_diag, -jnp.inf)
        m = jnp.maximum(jnp.max(s_diag, axis=-1, keepdims=True),
                        jnp.max(s_meta, axis=-1, keepdims=True))
        if r0:
            s_left = lax.dot_general(q, k_ref[0, :r0, :], _NT, preferred_element_type=F32)
            m = jnp.maximum(m, jnp.max(s_left, axis=-1, keepdims=True))
            p_left = jnp.exp(s_left - m)
        p_diag = jnp.exp(s_diag - m)
        p_meta = jnp.exp(s_meta - m)
        l = jnp.sum(p_diag, axis=-1, keepdims=True) + jnp.sum(p_meta, axis=-1, keepdims=True)
        o = (jnp.dot(p_diag.astype(BF16), v_ref[0, r0:r0 + tq, :], preferred_element_type=F32)
             + jnp.dot(p_meta.astype(BF16), vm, preferred_element_type=F32))
        if r0:
            l = l + jnp.sum(p_left, axis=-1, keepdims=True)
            o = o + jnp.dot(p_left.astype(BF16), v_ref[0, :r0, :], preferred_element_type=F32)
        o_ref[r0:r0 + tq, :] = (o / l).astype(BF16)


def _balanced_halves(n_tiles):
    assert n_tiles % 4 == 0
    pairs = [(i, n_tiles - 1 - i) for i in range(n_tiles // 2)]
    return (tuple(t for p in pairs[0::2] for t in p), tuple(t for p in pairs[1::2] for t in p))


def _mix_project(x, attn_o, mixed_fn, wpool_ref, pscale_ref, wo_ref, gm_ref, h_ref, hn_ref):
    parts = [attn_o]
    for g in range(len(POOL_WINDOWS)):
        lo = g * POOL_GROUP
        y = jnp.dot(mixed_fn(g).astype(BF16), wpool_ref[g], preferred_element_type=F32)
        parts.append((y * pscale_ref[:, lo:lo + POOL_GROUP]).astype(BF16))
    h = x + jnp.dot(jnp.concatenate(parts, axis=1), wo_ref[...], preferred_element_type=F32)
    h_ref[...] = h
    hn_ref[...] = _rms(h, gm_ref[...]).astype(BF16)


def _mix_out_kernel(x_ref, o_ref, pool_ref, halo_ref, pmeta_ref, wpool_ref, pscale_ref, wo_ref,
                    gm_ref, w32_ref, h_ref, hn_ref, w16_ref, ext_ref, *, tm, tiles_per_seq):
    w16_ref[...] = w32_ref[...].astype(BF16)

    i = pl.program_id(0)
    first = (i % tiles_per_seq) == 0
    ext_ref[0:N_META, :] = jnp.where(first, pmeta_ref[...], halo_ref[...])
    ext_ref[N_META:, :] = pool_ref[...]

    def mixed(g):
        w = POOL_WINDOWS[g]
        lo = g * POOL_GROUP
        cur = ext_ref[N_META:N_META + tm, lo:lo + POOL_GROUP]
        tot = cur
        for kk in range(1, w):
            tot = tot + ext_ref[N_META - kk:N_META - kk + tm, lo:lo + POOL_GROUP]
        return tot / float(w) - cur

    _mix_project(x_ref[...], o_ref[...], mixed, wpool_ref, pscale_ref, wo_ref, gm_ref,
                 h_ref, hn_ref)


def _mix_out(x, o, pool, pool_meta, wpool, pscale, wo, gm, w_cast, tm, seq):
    t = x.shape[0]
    steps = t // tm
    assert seq % tm == 0 and tm % N_META == 0 and N_META > POOL_HIST
    assert w_cast.shape[0] % steps == 0
    row = lambda w: pl.BlockSpec((tm, w), lambda i: (i, 0))
    cast_spec = pl.BlockSpec((w_cast.shape[0] // steps, w_cast.shape[1]), lambda i: (i, 0))
    hb = tm // N_META
    return pl.pallas_call(
        partial(_mix_out_kernel, tm=tm, tiles_per_seq=seq // tm),
        grid=(steps,),
        in_specs=[row(D_MODEL), row(MLA_WIDTH), row(POOL_WIDTH),
                  pl.BlockSpec((N_META, POOL_WIDTH), lambda i: (jnp.maximum(i * hb - 1, 0), 0)),
                  _const_spec((N_META, POOL_WIDTH)),
                  _const_spec((len(POOL_WINDOWS), POOL_GROUP, POOL_GROUP)),
                  _const_spec((1, POOL_WIDTH)), _const_spec((D_MODEL, D_MODEL)),
                  _const_spec((1, D_MODEL)), cast_spec],
        out_specs=[row(D_MODEL), row(D_MODEL), cast_spec],
        out_shape=[jax.ShapeDtypeStruct((t, D_MODEL), F32),
                   jax.ShapeDtypeStruct((t, D_MODEL), BF16),
                   jax.ShapeDtypeStruct(w_cast.shape, BF16)],
        scratch_shapes=[pltpu.VMEM((tm + N_META, POOL_WIDTH), F32)],
        compiler_params=pltpu.CompilerParams(
            dimension_semantics=("arbitrary",), vmem_limit_bytes=VMEM_LIMIT),
        name="mix_out",
    )(x, o, pool, pool, pool_meta, wpool, pscale, wo, gm, w_cast)


def _mix_out_sample_kernel(x_ref, lat_ref, wv_ref, pin_ref, st_ref, wpool_ref, pscale_ref, wo_ref,
                           gm_ref, h_ref, hn_ref):
    attn_o = jnp.concatenate(
        [jnp.dot(lat_ref[:, hh, :].astype(BF16), wv_ref[:, hh * V_DIM:(hh + 1) * V_DIM],
                 preferred_element_type=F32).astype(BF16) for hh in range(N_HEADS)], axis=1)

    def mixed(g):
        w = POOL_WINDOWS[g]
        lo = g * POOL_GROUP
        cur = pin_ref[:, lo:lo + POOL_GROUP]
        tot = cur
        for kk in range(1, w):
            base = (POOL_HIST - kk) * POOL_WIDTH + lo
            tot = tot + st_ref[:, base:base + POOL_GROUP]
        return tot / float(w) - cur

    _mix_project(x_ref[...], attn_o, mixed, wpool_ref, pscale_ref, wo_ref, gm_ref, h_ref, hn_ref)


def _mix_out_sample(x, lat, wv, pin, state, wpool, pscale, wo, gm):
    t = x.shape[0]
    full = lambda a: pl.BlockSpec(a.shape, lambda: (0,) * a.ndim)
    args = (x, lat, wv, pin, state, wpool, pscale, wo, gm)
    return pl.pallas_call(
        _mix_out_sample_kernel,
        in_specs=[full(a) for a in args],
        out_specs=[pl.BlockSpec((t, D_MODEL), lambda: (0, 0))] * 2,
        out_shape=[jax.ShapeDtypeStruct((t, D_MODEL), F32),
                   jax.ShapeDtypeStruct((t, D_MODEL), BF16)],
        compiler_params=pltpu.CompilerParams(vmem_limit_bytes=VMEM_LIMIT),
        name="mix_out_sample",
    )(*args)


def _mlp_kernel(h_ref, hn_ref, wu_ref, wd_ref, gf_ref, y_ref):
    f = pl.program_id(1)

    @pl.when(f == 0)
    def _():
        y_ref[...] = h_ref[...]

    z = jnp.dot(hn_ref[...], wu_ref[...], preferred_element_type=F32)
    a = jnp.square(jnp.maximum(z, 0.0)).astype(BF16)
    y_ref[...] += jnp.dot(a, wd_ref[...], preferred_element_type=F32)

    @pl.when(f == pl.num_programs(1) - 1)
    def _():
        y_ref[...] = _rms(y_ref[...], gf_ref[...])


def _mlp(h, hn, wu, wd, gf, tm, tf):
    t = h.shape[0]
    assert t % tm == 0 and D_FF % tf == 0
    return pl.pallas_call(
        _mlp_kernel,
        grid=(t // tm, D_FF // tf),
        in_specs=[pl.BlockSpec((tm, D_MODEL), lambda i, f: (i, 0)),
                  pl.BlockSpec((tm, D_MODEL), lambda i, f: (i, 0)),
                  pl.BlockSpec((D_MODEL, tf), lambda i, f: (0, f)),
                  pl.BlockSpec((tf, D_MODEL), lambda i, f: (f, 0)),
                  pl.BlockSpec((1, D_MODEL), lambda i, f: (0, 0))],
        out_specs=pl.BlockSpec((tm, D_MODEL), lambda i, f: (i, 0)),
        out_shape=jax.ShapeDtypeStruct((t, D_MODEL), F32),
        compiler_params=pltpu.CompilerParams(
            dimension_semantics=("arbitrary", "arbitrary"), vmem_limit_bytes=VMEM_LIMIT),
        name="mlp",
    )(h, hn, wu, wd, gf)


def _absorb_kernel(q_ref, wk_ref, o_ref, *, db):
    for hh in range(N_HEADS):
        q = q_ref[hh, :db, :]
        qlat = lax.dot_general(q[:, :LANES], wk_ref[:, hh * LANES:(hh + 1) * LANES], _NT,
                               preferred_element_type=F32)
        o_ref[:, hh, :KV_RANK] = qlat
        o_ref[:, hh, KV_RANK:] = q[:, LANES:].astype(F32)


def _absorb(q, wk, db):
    full = lambda a: pl.BlockSpec(a.shape, lambda: (0,) * a.ndim)
    return pl.pallas_call(
        partial(_absorb_kernel, db=db),
        in_specs=[full(q), full(wk)],
        out_specs=pl.BlockSpec((db, N_HEADS, QCAT), lambda: (0, 0, 0)),
        out_shape=jax.ShapeDtypeStruct((db, N_HEADS, QCAT), F32),
        compiler_params=pltpu.CompilerParams(vmem_limit_bytes=VMEM_LIMIT),
        name="absorb_q",
    )(q, wk)


def _decode_attend_kernel(pt_ref, q_ref, cown_ref, kown_ref, lat_hbm, krt_hbm,
                          qa_ref, ka_ref, va_ref, km_ref, vm_ref, o_ref, oa_ref,
                          latbuf, krtbuf, sem, *, gp, ng, nslot, wchunk, tq, halves, attn_steps):
    b = pl.program_id(0)
    nb = pl.num_programs(0)

    def page_copies(bb, gi, k):
        slot = gi % nslot
        page = pt_ref[bb, gi * gp + k]
        return (pltpu.make_async_copy(lat_hbm.at[page], latbuf.at[slot, k], sem.at[slot]),
                pltpu.make_async_copy(krt_hbm.at[page], krtbuf.at[slot, k], sem.at[slot]))

    def start(bb, gi):
        for k in range(gp):
            for cp in page_copies(bb, gi, k):
                cp.start()

    def wait(bb, gi):
        for k in range(gp):
            for cp in page_copies(bb, gi, k):
                cp.wait()

    def refill_target(gi):
        nxt = gi + nslot
        if nxt < ng:
            return b, nxt
        return jnp.minimum(b + 1, nb - 1), nxt - ng

    @pl.when(b == 0)
    def _():
        for gi in range(nslot):
            start(0, gi)

    q = q_ref[0]
    qlat = q[:, :KV_RANK]
    qpe = q[:, KV_RANK:KV_RANK + QK_ROPE]

    def scores(gi):
        slot = gi % nslot
        parts = [lax.dot_general(qlat, latbuf[slot, k], _NT, preferred_element_type=F32)
                 + jnp.dot(qpe, krtbuf[slot, k], preferred_element_type=F32)
                 for k in range(gp)]
        return jnp.concatenate(parts, axis=-1)

    def fold(gi, s, m, l, acc):
        slot = gi % nslot
        m_new = jnp.maximum(m, jnp.max(s, axis=-1, keepdims=True))
        corr = jnp.exp(m - m_new)
        p = jnp.exp(s - m_new)
        l = l * corr + jnp.sum(p, axis=-1, keepdims=True)
        acc = acc * corr
        for k in range(gp):
            acc = acc + jnp.dot(p[:, k * PAGE_SIZE:(k + 1) * PAGE_SIZE], latbuf[slot, k],
                                preferred_element_type=F32)
        start(*refill_target(gi))
        return m_new, l, acc

    c_own = cown_ref[0]
    k_own = kown_ref[0]
    m = (jnp.sum(qlat * c_own, axis=-1, keepdims=True)
         + jnp.sum(qpe * k_own[:, :QK_ROPE], axis=-1, keepdims=True))
    l = jnp.ones_like(m)
    acc = jnp.broadcast_to(c_own, (N_HEADS, KV_RANK))

    for gw in range(wchunk):
        wait(b, gw)
    s_cur = scores(0)
    for gi in range(ng):
        if gi + 1 < ng:
            if (gi + 1) % wchunk == 0:
                for gw in range(gi + 1, gi + 1 + wchunk):
                    wait(b, gw)
            s_next = scores(gi + 1)
        m, l, acc = fold(gi, s_cur, m, l, acc)
        if gi + 1 < ng:
            s_cur = s_next
    o_ref[0] = acc / l

    for half, tiles in enumerate(halves):
        @pl.when((b < attn_steps) & (b % len(halves) == half))
        def _(tiles=tiles):
            _attn_tiles(qa_ref, ka_ref, va_ref, km_ref, vm_ref, oa_ref, tiles, tq)

    @pl.when(b == nb - 1)
    def _():
        for gi in range(nslot):
            wait(nb - 1, gi)


def _decode_attend(page_table, qcat, c_own, k_own, cache_lat, cache_krt,
                   q_x, k_x, v_x, k_meta, v_meta, batch, seq):
    db, n_pages = page_table.shape
    gp, nslot, wchunk, tq = DECODE_GROUP_PAGES, DECODE_SLOTS, DECODE_WAIT_GROUPS, TQ_ATTN
    ng = n_pages // gp
    assert n_pages % gp == 0 and ng % nslot == 0 and seq % tq == 0
    assert ng % wchunk == 0 and wchunk < nslot
    n_units = batch * N_HEADS
    halves = _balanced_halves(seq // tq)
    attn_steps = len(halves) * n_units
    assert attn_steps <= db

    def unit(b):
        u = jnp.minimum(b // len(halves), n_units - 1)
        return u % N_HEADS, u // N_HEADS

    per_seq = lambda shape: pl.BlockSpec(shape, lambda b, pt: (b, 0, 0))
    a_seq = lambda w: pl.BlockSpec((1, seq, w), lambda b, pt: (*unit(b), 0))
    a_meta = lambda w: pl.BlockSpec((1, N_META, w), lambda b, pt: (unit(b)[0], 0, 0))
    grid_spec = pltpu.PrefetchScalarGridSpec(
        num_scalar_prefetch=1,
        grid=(db,),
        in_specs=[per_seq((1, N_HEADS, QCAT)), per_seq((1, 1, KV_RANK)), per_seq((1, 1, LANES)),
                  pl.BlockSpec(memory_space=pl.ANY), pl.BlockSpec(memory_space=pl.ANY),
                  a_seq(HEAD_PAD), a_seq(HEAD_PAD), a_seq(V_DIM), a_meta(HEAD_PAD), a_meta(V_DIM)],
        out_specs=[per_seq((1, N_HEADS, KV_RANK)),
                   pl.BlockSpec((seq, V_DIM), lambda b, pt: unit(b)[::-1])],
        scratch_shapes=[pltpu.VMEM((nslot, gp, PAGE_SIZE, KV_RANK), F32),
                        pltpu.VMEM((nslot, gp, QK_ROPE, PAGE_SIZE), F32),
                        pltpu.SemaphoreType.DMA((nslot,))],
    )
    return pl.pallas_call(
        partial(_decode_attend_kernel, gp=gp, ng=ng, nslot=nslot, wchunk=wchunk, tq=tq, halves=halves,
                attn_steps=attn_steps),
        grid_spec=grid_spec,
        out_shape=[jax.ShapeDtypeStruct((db, N_HEADS, KV_RANK), F32),
                   jax.ShapeDtypeStruct((batch * seq, MLA_WIDTH), BF16)],
        compiler_params=pltpu.CompilerParams(
            dimension_semantics=("arbitrary",), vmem_limit_bytes=VMEM_LIMIT),
        name="paged_decode_prompt_attn",
    )(page_table, qcat, c_own, k_own, cache_lat, cache_krt, q_x, k_x, v_x, k_meta, v_meta)


def _rot_cols(w):
    half = w.shape[-1] // 2
    return jnp.concatenate([-w[..., half:], w[..., :half]], axis=-1)


def _rope_tables(pos):
    inv = ROPE_THETA ** (-np.arange(0, QK_ROPE, 2, dtype=np.float64) / QK_ROPE)
    ang = np.asarray(pos, np.float64)[:, None] * inv[None, :]
    ang = np.concatenate([ang, ang], axis=-1)
    z = np.zeros_like(ang)
    return (jnp.asarray(np.concatenate([np.cos(ang), z], axis=-1), F32),
            jnp.asarray(np.concatenate([np.sin(ang), z], axis=-1), F32))


def kernel(x_prompt, x_sample, cache_kv_latent, cache_k_rope, state_pool, page_table, meta_tokens,
           g_attn, w_in, g_q, w_q_b, g_kv, w_kv_b, w_pool, pool_scale, w_o, g_mlp, w_up, w_down,
           g_final):
    assert g_attn.shape[0] == 1, "single-layer trunk"
    batch, seq, _ = x_prompt.shape
    db, s1, _ = x_sample.shape
    assert s1 == 1
    assert seq >= POOL_HIST

    kr_lo = Q_RANK + KV_RANK
    win = w_in[0]
    win_r = jnp.concatenate([win[:, :kr_lo + QK_ROPE], _rot_cols(win[:, kr_lo:kr_lo + QK_ROPE]),
                             win[:, kr_lo + QK_ROPE:]], axis=1).astype(BF16)
    wq = w_q_b[0].reshape(Q_RANK, N_HEADS, QK_NOPE + QK_ROPE)
    wq_r = jnp.concatenate([wq, _rot_cols(wq[..., QK_NOPE:])], axis=-1)
    wq_r = wq_r.reshape(Q_RANK, N_HEADS * HEAD_PAD).astype(BF16)
    wk = w_kv_b[0][..., :QK_NOPE].reshape(KV_RANK, MLA_WIDTH).astype(BF16)
    wv = w_kv_b[0][..., QK_NOPE:].reshape(KV_RANK, MLA_WIDTH).astype(BF16)
    wpool = w_pool[0].astype(BF16)
    ga, gq, gkv, gm = g_attn, g_q, g_kv, g_mlp
    gf = g_final[None, :]
    pscale = pool_scale

    mix_w = (ga, win_r, gq, wq_r, gkv, wk, wv)
    x = x_prompt.reshape(batch * seq, D_MODEL)
    cos_x, sin_x = _rope_tables(N_META + np.arange(seq))
    lat_x, kpe_x, pool_x, q_x, k_x, v_x, wo, wu = _mixer_in(
        x, cos_x, sin_x, *mix_w, tm=TM_MIX, cast=(w_o[0], w_up[0]))
    xs = x_sample.reshape(db, D_MODEL)
    x_sm = jnp.concatenate([xs, meta_tokens], axis=0)
    cos_sm, sin_sm = _rope_tables(np.concatenate([np.full(db, PAST_LEN), np.arange(N_META)]))
    lat_sm, kpe_sm, pool_sm, q_sm, k_sm, v_sm = _mixer_in(x_sm, cos_sm, sin_sm, *mix_w,
                                                          tm=db + N_META)
    lat_s, kpe_s, pool_s = lat_sm[:db], kpe_sm[:db], pool_sm[:db]
    lat_m, kpe_m, pool_m = lat_sm[db:], kpe_sm[db:], pool_sm[db:]

    qcat = _absorb(q_sm, wk, db)
    n_phys = cache_kv_latent.shape[1]
    dec, o_x = _decode_attend(
        page_table, qcat, lat_s.reshape(db, 1, KV_RANK), kpe_s.reshape(db, 1, LANES),
        cache_kv_latent.reshape(n_phys, PAGE_SIZE, KV_RANK),
        jnp.swapaxes(cache_k_rope[0], 1, 2),
        q_x, k_x, v_x, k_sm[:, db:], v_sm[:, db:], batch, seq)

    h_x, hn_x, wd = _mix_out(x, o_x, pool_x, pool_m, wpool, pscale, wo, gm, w_down[0],
                             tm=TM_OUT, seq=seq)
    y_x = _mlp(h_x, hn_x, wu, wd, gf, tm=TM_MLP, tf=TF_MLP)

    per_seq = lambda a: a.reshape(batch, seq, a.shape[-1])
    with_meta = lambda m, a: jnp.concatenate(
        [jnp.broadcast_to(m[None], (batch,) + m.shape), per_seq(a)], axis=1)[None]
    y_prompt = per_seq(y_x)
    new_lat_prompt = with_meta(lat_m, lat_x)
    new_krope_prompt = with_meta(kpe_m[:, :QK_ROPE], kpe_x[:, :QK_ROPE])
    new_pool_prompt = per_seq(pool_x)[None, :, seq - POOL_HIST:]

    state = state_pool[0]
    h_s, hn_s = _mix_out_sample(xs, dec, wv, pool_s, state.reshape(db, POOL_HIST * POOL_WIDTH),
                                wpool, pscale, wo, gm)
    y_s = _mlp(h_s, hn_s, wu, wd, gf, tm=db, tf=TF_MLP)

    y_sample = y_s.reshape(db, 1, D_MODEL)
    new_lat_sample = lat_s.reshape(1, db, 1, KV_RANK)
    new_krope_sample = kpe_s[:, :QK_ROPE].reshape(1, db, 1, QK_ROPE)
    new_pool_sample = jnp.concatenate([state[:, 1:], pool_s[:, None, :]], axis=1)[None]
    return (y_prompt, y_sample, new_lat_prompt, new_krope_prompt, new_pool_prompt,
            new_lat_sample, new_krope_sample, new_pool_sample)
```

```python
from functools import partial

import jax
import jax.numpy as jnp
import numpy as np
from jax import lax
from jax.experimental import pallas as pl
from jax.experimental.pallas import tpu as pltpu

D_MODEL = 2048
N_META = 16
N_HEADS = 12
QK_NOPE = 128
QK_ROPE = 64
V_DIM = 128
Q_RANK = 512
KV_RANK = 512
MLA_WIDTH = N_HEADS * V_DIM
POOL_WIDTH = D_MODEL - MLA_WIDTH
POOL_WINDOWS = (2, 4, 8, 16)
POOL_GROUP = 128
POOL_HIST = 15
D_FF = 4 * D_MODEL
ROPE_THETA = 10000.0
NORM_EPS = 1e-6
PAST_LEN = 16384
PAGE_SIZE = 128

LANES = 128
HEAD_PAD = 256
QCAT = 640
IN_COLS_R = Q_RANK + KV_RANK + 2 * QK_ROPE + POOL_WIDTH
SCALE = (QK_NOPE + QK_ROPE) ** -0.5
VMEM_LIMIT = 56 * 1024 * 1024
TM_MIX = 256
TM_OUT = 512
TQ_ATTN = 256
TM_MLP = 512
TF_MLP = 1024
DECODE_GROUP_PAGES = 16
DECODE_SLOTS = 8
DECODE_WAIT_GROUPS = 4

BF16 = jnp.bfloat16
F32 = jnp.float32


def _rms(x, g):
    return x * lax.rsqrt(jnp.mean(x * x, axis=-1, keepdims=True) + NORM_EPS) * g


def _rope128(x, c, s):
    return x * c + pltpu.roll(x, QK_ROPE, axis=1) * s


def _const_spec(shape):
    n = len(shape)
    return pl.BlockSpec(shape, lambda *_: (0,) * n, pipeline_mode=pl.Buffered(1))


def _mixer_in_kernel(x_ref, ga_ref, win_ref, gq_ref, wq_ref, gkv_ref, wk_ref, wv_ref,
                     c_ref, s_ref, *rest, n_cast):
    cast_in, rest = rest[:n_cast], rest[n_cast:]
    lat_ref, kpe_ref, pool_ref, q_ref, k_ref, v_ref = rest[:6]
    for w32_ref, w16_ref in zip(cast_in, rest[6:]):
        w16_ref[...] = w32_ref[...].astype(BF16)

    h = _rms(x_ref[...], ga_ref[...]).astype(BF16)
    proj = jnp.dot(h, win_ref[...], preferred_element_type=F32)
    cos = c_ref[...]
    sin = s_ref[...]
    ckv = _rms(proj[:, Q_RANK:Q_RANK + KV_RANK], gkv_ref[...])
    lat_ref[...] = ckv
    kpe = _rope128(proj[:, 2 * Q_RANK:2 * Q_RANK + LANES], cos, sin)
    kpe_ref[...] = kpe
    pool_ref[...] = proj[:, 2 * Q_RANK + LANES:]
    kpe_b = kpe.astype(BF16)

    qn = _rms(proj[:, :Q_RANK], gq_ref[...]).astype(BF16)
    q = jnp.dot(qn, wq_ref[...], preferred_element_type=F32)
    for hh in range(N_HEADS):
        base = hh * HEAD_PAD
        q_ref[hh, :, :LANES] = (q[:, base:base + LANES] * SCALE).astype(BF16)
        pe = _rope128(q[:, base + LANES:base + HEAD_PAD], cos, sin) * SCALE
        q_ref[hh, :, LANES:] = pe.astype(BF16)

    cb = ckv.astype(BF16)
    kn = jnp.dot(cb, wk_ref[...], preferred_element_type=F32)
    vv = jnp.dot(cb, wv_ref[...], preferred_element_type=F32)
    for hh in range(N_HEADS):
        k_ref[hh, :, :LANES] = kn[:, hh * LANES:(hh + 1) * LANES].astype(BF16)
        k_ref[hh, :, LANES:] = kpe_b
        v_ref[hh] = vv[:, hh * LANES:(hh + 1) * LANES].astype(BF16)


def _mixer_in(x, cos, sin, ga, win, gq, wq, gkv, wk, wv, tm, cast=()):
    t = x.shape[0]
    steps = t // tm
    table_blocks = cos.shape[0] // tm
    assert t % tm == 0 and cos.shape[0] % tm == 0
    assert all(w.shape[0] % steps == 0 for w in cast)
    row = lambda w: pl.BlockSpec((tm, w), lambda i: (i, 0))
    table = pl.BlockSpec((tm, LANES), lambda i: (i % table_blocks, 0))
    head = lambda w: pl.BlockSpec((N_HEADS, tm, w), lambda i: (0, i, 0))
    cast_specs = [pl.BlockSpec((w.shape[0] // steps, w.shape[1]), lambda i: (i, 0)) for w in cast]
    return pl.pallas_call(
        partial(_mixer_in_kernel, n_cast=len(cast)),
        grid=(steps,),
        in_specs=[row(D_MODEL), _const_spec((1, D_MODEL)), _const_spec((D_MODEL, IN_COLS_R)),
                  _const_spec((1, Q_RANK)), _const_spec((Q_RANK, N_HEADS * HEAD_PAD)),
                  _const_spec((1, KV_RANK)), _const_spec((KV_RANK, MLA_WIDTH)),
                  _const_spec((KV_RANK, MLA_WIDTH)), table, table] + cast_specs,
        out_specs=[row(KV_RANK), row(LANES), row(POOL_WIDTH),
                   head(HEAD_PAD), head(HEAD_PAD), head(V_DIM)] + cast_specs,
        out_shape=[jax.ShapeDtypeStruct((t, KV_RANK), F32),
                   jax.ShapeDtypeStruct((t, LANES), F32),
                   jax.ShapeDtypeStruct((t, POOL_WIDTH), F32),
                   jax.ShapeDtypeStruct((N_HEADS, t, HEAD_PAD), BF16),
                   jax.ShapeDtypeStruct((N_HEADS, t, HEAD_PAD), BF16),
                   jax.ShapeDtypeStruct((N_HEADS, t, V_DIM), BF16)]
                  + [jax.ShapeDtypeStruct(w.shape, BF16) for w in cast],
        compiler_params=pltpu.CompilerParams(
            dimension_semantics=("arbitrary",), vmem_limit_bytes=VMEM_LIMIT),
        name="mixer_in",
    )(x, ga, win, gq, wq, gkv, wk, wv, cos, sin, *cast)


_NT = (((1,), (1,)), ((), ()))


def _attn_tiles(q_ref, k_ref, v_ref, km_ref, vm_ref, o_ref, tiles, tq):
    km = km_ref[0]
    vm = vm_ref[0]
    row = lax.broadcasted_iota(jnp.int32, (tq, tq), 0)
    col = lax.broadcasted_iota(jnp.int32, (tq, tq), 1)
    for ti in tiles:
        r0 = ti * tq
        q = q_ref[0, r0:r0 + tq, :]
        s_meta = lax.dot_general(q, km, _NT, preferred_element_type=F32)
        s_diag = lax.dot_general(q, k_ref[0, r0:r0 + tq, :], _NT, preferred_element_type=F32)
        s_diag = jnp.where(col <= row, s_diag, -jnp.inf)
        m = jnp.maximum(jnp.max(s_diag, axis=-1, keepdims=True),
                        jnp.max(s_meta, axis=-1, keepdims=True))
        if r0:
            s_left = lax.dot_general(q, k_ref[0, :r0, :], _NT, preferred_element_type=F32)
            m = jnp.maximum(m, jnp.max(s_left, axis=-1, keepdims=True))
            p_left = jnp.exp(s_left - m)
        p_diag = jnp.exp(s_diag - m)
        p_meta = jnp.exp(s_meta - m)
        l = jnp.sum(p_diag, axis=-1, keepdims=True) + jnp.sum(p_meta, axis=-1, keepdims=True)
        o = (jnp.dot(p_diag.astype(BF16), v_ref[0, r0:r0 + tq, :], preferred_element_type=F32)
             + jnp.dot(p_meta.astype(BF16), vm, preferred_element_type=F32))
        if r0:
            l = l + jnp.sum(p_left, axis=-1, keepdims=True)
            o = o + jnp.dot(p_left.astype(BF16), v_ref[0, :r0, :], preferred_element_type=F32)
        o_ref[r0:r0 + tq, :] = (o / l).astype(BF16)


def _balanced_halves(n_tiles):
    assert n_tiles % 4 == 0
    pairs = [(i, n_tiles - 1 - i) for i in range(n_tiles // 2)]
    return (tuple(t for p in pairs[0::2] for t in p), tuple(t for p in pairs[1::2] for t in p))


def _mix_project(x, attn_o, mixed_fn, wpool_ref, pscale_ref, wo_ref, gm_ref, h_ref, hn_ref):
    parts = [attn_o]
    for g in range(len(POOL_WINDOWS)):
        lo = g * POOL_GROUP
        y = jnp.dot(mixed_fn(g).astype(BF16), wpool_ref[g], preferred_element_type=F32)
        parts.append((y * pscale_ref[:, lo:lo + POOL_GROUP]).astype(BF16))
    h = x + jnp.dot(jnp.concatenate(parts, axis=1), wo_ref[...], preferred_element_type=F32)
    h_ref[...] = h
    hn_ref[...] = _rms(h, gm_ref[...]).astype(BF16)


def _mix_out_kernel(x_ref, o_ref, pool_ref, halo_ref, pmeta_ref, wpool_ref, pscale_ref, wo_ref,
                    gm_ref, w32_ref, h_ref, hn_ref, w16_ref, ext_ref, *, tm, tiles_per_seq):
    w16_ref[...] = w32_ref[...].astype(BF16)

    i = pl.program_id(0)
    first = (i % tiles_per_seq) == 0
    ext_ref[0:N_META, :] = jnp.where(first, pmeta_ref[...], halo_ref[...])
    ext_ref[N_META:, :] = pool_ref[...]

    def mixed(g):
        w = POOL_WINDOWS[g]
        lo = g * POOL_GROUP
        cur = ext_ref[N_META:N_META + tm, lo:lo + POOL_GROUP]
        tot = cur
        for kk in range(1, w):
            tot = tot + ext_ref[N_META - kk:N_META - kk + tm, lo:lo + POOL_GROUP]
        return tot / float(w) - cur

    _mix_project(x_ref[...], o_ref[...], mixed, wpool_ref, pscale_ref, wo_ref, gm_ref,
                 h_ref, hn_ref)


def _mix_out(x, o, pool, pool_meta, wpool, pscale, wo, gm, w_cast, tm, seq):
    t = x.shape[0]
    steps = t // tm
    assert seq % tm == 0 and tm % N_META == 0 and N_META > POOL_HIST
    assert w_cast.shape[0] % steps == 0
    row = lambda w: pl.BlockSpec((tm, w), lambda i: (i, 0))
    cast_spec = pl.BlockSpec((w_cast.shape[0] // steps, w_cast.shape[1]), lambda i: (i, 0))
    hb = tm // N_META
    return pl.pallas_call(
        partial(_mix_out_kernel, tm=tm, tiles_per_seq=seq // tm),
        grid=(steps,),
        in_specs=[row(D_MODEL), row(MLA_WIDTH), row(POOL_WIDTH),
                  pl.BlockSpec((N_META, POOL_WIDTH), lambda i: (jnp.maximum(i * hb - 1, 0), 0)),
                  _const_spec((N_META, POOL_WIDTH)),
                  _const_spec((len(POOL_WINDOWS), POOL_GROUP, POOL_GROUP)),
                  _const_spec((1, POOL_WIDTH)), _const_spec((D_MODEL, D_MODEL)),
                  _const_spec((1, D_MODEL)), cast_spec],
        out_specs=[row(D_MODEL), row(D_MODEL), cast_spec],
        out_shape=[jax.ShapeDtypeStruct((t, D_MODEL), F32),
                   jax.ShapeDtypeStruct((t, D_MODEL), BF16),
                   jax.ShapeDtypeStruct(w_cast.shape, BF16)],
        scratch_shapes=[pltpu.VMEM((tm + N_META, POOL_WIDTH), F32)],
        compiler_params=pltpu.CompilerParams(
            dimension_semantics=("arbitrary",), vmem_limit_bytes=VMEM_LIMIT),
        name="mix_out",
    )(x, o, pool, pool, pool_meta, wpool, pscale, wo, gm, w_cast)


def _mix_out_sample_kernel(x_ref, lat_ref, wv_ref, pin_ref, st_ref, wpool_ref, pscale_ref, wo_ref,
                           gm_ref, h_ref, hn_ref):
    attn_o = jnp.concatenate(
        [jnp.dot(lat_ref[:, hh, :].astype(BF16), wv_ref[:, hh * V_DIM:(hh + 1) * V_DIM],
                 preferred_element_type=F32).astype(BF16) for hh in range(N_HEADS)], axis=1)

    def mixed(g):
        w = POOL_WINDOWS[g]
        lo = g * POOL_GROUP
        cur = pin_ref[:, lo:lo + POOL_GROUP]
        tot = cur
        for kk in range(1, w):
            tot = tot + st_ref[:, POOL_HIST - kk, lo:lo + POOL_GROUP]
        return tot / float(w) - cur

    _mix_project(x_ref[...], attn_o, mixed, wpool_ref, pscale_ref, wo_ref, gm_ref, h_ref, hn_ref)


def _mix_out_sample(x, lat, wv, pin, state, wpool, pscale, wo, gm):
    t = x.shape[0]
    full = lambda a: pl.BlockSpec(a.shape, lambda: (0,) * a.ndim)
    args = (x, lat, wv, pin, state, wpool, pscale, wo, gm)
    return pl.pallas_call(
        _mix_out_sample_kernel,
        in_specs=[full(a) for a in args],
        out_specs=[pl.BlockSpec((t, D_MODEL), lambda: (0, 0))] * 2,
        out_shape=[jax.ShapeDtypeStruct((t, D_MODEL), F32),
                   jax.ShapeDtypeStruct((t, D_MODEL), BF16)],
        compiler_params=pltpu.CompilerParams(vmem_limit_bytes=VMEM_LIMIT),
        name="mix_out_sample",
    )(*args)


def _mlp_kernel(h_ref, hn_ref, wu_ref, wd_ref, gf_ref, y_ref):
    f = pl.program_id(1)

    @pl.when(f == 0)
    def _():
        y_ref[...] = h_ref[...]

    z = jnp.dot(hn_ref[...], wu_ref[...], preferred_element_type=F32)
    a = jnp.square(jnp.maximum(z, 0.0)).astype(BF16)
    y_ref[...] += jnp.dot(a, wd_ref[...], preferred_element_type=F32)

    @pl.when(f == pl.num_programs(1) - 1)
    def _():
        y_ref[...] = _rms(y_ref[...], gf_ref[...])


def _mlp(h, hn, wu, wd, gf, tm, tf):
    t = h.shape[0]
    assert t % tm == 0 and D_FF % tf == 0
    return pl.pallas_call(
        _mlp_kernel,
        grid=(t // tm, D_FF // tf),
        in_specs=[pl.BlockSpec((tm, D_MODEL), lambda i, f: (i, 0)),
                  pl.BlockSpec((tm, D_MODEL), lambda i, f: (i, 0)),
                  pl.BlockSpec((D_MODEL, tf), lambda i, f: (0, f)),
                  pl.BlockSpec((tf, D_MODEL), lambda i, f: (f, 0)),
                  pl.BlockSpec((1, D_MODEL), lambda i, f: (0, 0))],
        out_specs=pl.BlockSpec((tm, D_MODEL), lambda i, f: (i, 0)),
        out_shape=jax.ShapeDtypeStruct((t, D_MODEL), F32),
        compiler_params=pltpu.CompilerParams(
            dimension_semantics=("arbitrary", "arbitrary"), vmem_limit_bytes=VMEM_LIMIT),
        name="mlp",
    )(h, hn, wu, wd, gf)


def _absorb_kernel(q_ref, wk_ref, o_ref, *, db):
    for hh in range(N_HEADS):
        q = q_ref[hh, :db, :]
        qlat = lax.dot_general(q[:, :LANES], wk_ref[:, hh * LANES:(hh + 1) * LANES], _NT,
                               preferred_element_type=F32)
        o_ref[:, hh, :KV_RANK] = qlat
        o_ref[:, hh, KV_RANK:] = q[:, LANES:].astype(F32)


def _absorb(q, wk, db):
    full = lambda a: pl.BlockSpec(a.shape, lambda: (0,) * a.ndim)
    return pl.pallas_call(
        partial(_absorb_kernel, db=db),
        in_specs=[full(q), full(wk)],
        out_specs=pl.BlockSpec((db, N_HEADS, QCAT), lambda: (0, 0, 0)),
        out_shape=jax.ShapeDtypeStruct((db, N_HEADS, QCAT), F32),
        compiler_params=pltpu.CompilerParams(vmem_limit_bytes=VMEM_LIMIT),
        name="absorb_q",
    )(q, wk)


def _decode_attend_kernel(pt_ref, q_ref, cown_ref, kown_ref, lat_hbm, krt_hbm,
                          qa_ref, ka_ref, va_ref, km_ref, vm_ref, o_ref, oa_ref,
                          latbuf, krtbuf, sem, *, gp, ng, nslot, wchunk, tq, halves, attn_steps):
    b = pl.program_id(0)
    nb = pl.num_programs(0)

    def page_copies(bb, gi, k):
        slot = gi % nslot
        page = pt_ref[bb, gi * gp + k]
        return (pltpu.make_async_copy(lat_hbm.at[page], latbuf.at[slot, k], sem.at[slot]),
                pltpu.make_async_copy(krt_hbm.at[page], krtbuf.at[slot, k], sem.at[slot]))

    def start(bb, gi):
        for k in range(gp):
            for cp in page_copies(bb, gi, k):
                cp.start()

    def wait(bb, gi):
        for k in range(gp):
            for cp in page_copies(bb, gi, k):
                cp.wait()

    def refill_target(gi):
        nxt = gi + nslot
        if nxt < ng:
            return b, nxt
        return jnp.minimum(b + 1, nb - 1), nxt - ng

    @pl.when(b == 0)
    def _():
        for gi in range(nslot):
            start(0, gi)

    q = q_ref[0]
    qlat = q[:, :KV_RANK]
    qpe = q[:, KV_RANK:KV_RANK + QK_ROPE]

    def scores(gi):
        slot = gi % nslot
        parts = [lax.dot_general(qlat, latbuf[slot, k], _NT, preferred_element_type=F32)
                 + jnp.dot(qpe, krtbuf[slot, k], preferred_element_type=F32)
                 for k in range(gp)]
        return jnp.concatenate(parts, axis=-1)

    def fold(gi, s, m, l, acc):
        slot = gi % nslot
        m_new = jnp.maximum(m, jnp.max(s, axis=-1, keepdims=True))
        corr = jnp.exp(m - m_new)
        p = jnp.exp(s - m_new)
        l = l * corr + jnp.sum(p, axis=-1, keepdims=True)
        acc = acc * corr
        for k in range(gp):
            acc = acc + jnp.dot(p[:, k * PAGE_SIZE:(k + 1) * PAGE_SIZE], latbuf[slot, k],
                                preferred_element_type=F32)
        start(*refill_target(gi))
        return m_new, l, acc

    c_own = cown_ref[0]
    k_own = kown_ref[0]
    m = (jnp.sum(qlat * c_own, axis=-1, keepdims=True)
         + jnp.sum(qpe * k_own[:, :QK_ROPE], axis=-1, keepdims=True))
    l = jnp.ones_like(m)
    acc = jnp.broadcast_to(c_own, (N_HEADS, KV_RANK))

    for gw in range(wchunk):
        wait(b, gw)
    s_cur = scores(0)
    for gi in range(ng):
        if gi + 1 < ng:
            if (gi + 1) % wchunk == 0:
                for gw in range(gi + 1, gi + 1 + wchunk):
                    wait(b, gw)
            s_next = scores(gi + 1)
        m, l, acc = fold(gi, s_cur, m, l, acc)
        if gi + 1 < ng:
            s_cur = s_next
    o_ref[0] = acc / l

    for half, tiles in enumerate(halves):
        @pl.when((b < attn_steps) & (b % len(halves) == half))
        def _(tiles=tiles):
            _attn_tiles(qa_ref, ka_ref, va_ref, km_ref, vm_ref, oa_ref, tiles, tq)

    @pl.when(b == nb - 1)
    def _():
        for gi in range(nslot):
            wait(nb - 1, gi)


def _decode_attend(page_table, qcat, c_own, k_own, cache_lat, cache_krt,
                   q_x, k_x, v_x, k_meta, v_meta, batch, seq):
    db, n_pages = page_table.shape
    gp, nslot, wchunk, tq = DECODE_GROUP_PAGES, DECODE_SLOTS, DECODE_WAIT_GROUPS, TQ_ATTN
    ng = n_pages // gp
    assert n_pages % gp == 0 and ng % nslot == 0 and seq % tq == 0
    assert ng % wchunk == 0 and wchunk < nslot
    n_units = batch * N_HEADS
    halves = _balanced_halves(seq // tq)
    attn_steps = len(halves) * n_units
    assert attn_steps <= db

    def unit(b):
        u = jnp.minimum(b // len(halves), n_units - 1)
        return u % N_HEADS, u // N_HEADS

    per_seq = lambda shape: pl.BlockSpec(shape, lambda b, pt: (b, 0, 0))
    a_seq = lambda w: pl.BlockSpec((1, seq, w), lambda b, pt: (*unit(b), 0))
    a_meta = lambda w: pl.BlockSpec((1, N_META, w), lambda b, pt: (unit(b)[0], 0, 0))
    grid_spec = pltpu.PrefetchScalarGridSpec(
        num_scalar_prefetch=1,
        grid=(db,),
        in_specs=[per_seq((1, N_HEADS, QCAT)), per_seq((1, 1, KV_RANK)), per_seq((1, 1, LANES)),
                  pl.BlockSpec(memory_space=pl.ANY), pl.BlockSpec(memory_space=pl.ANY),
                  a_seq(HEAD_PAD), a_seq(HEAD_PAD), a_seq(V_DIM), a_meta(HEAD_PAD), a_meta(V_DIM)],
        out_specs=[per_seq((1, N_HEADS, KV_RANK)),
                   pl.BlockSpec((seq, V_DIM), lambda b, pt: unit(b)[::-1])],
        scratch_shapes=[pltpu.VMEM((nslot, gp, PAGE_SIZE, KV_RANK), F32),
                        pltpu.VMEM((nslot, gp, QK_ROPE, PAGE_SIZE), F32),
                        pltpu.SemaphoreType.DMA((nslot,))],
    )
    return pl.pallas_call(
        partial(_decode_attend_kernel, gp=gp, ng=ng, nslot=nslot, wchunk=wchunk, tq=tq, halves=halves,
                attn_steps=attn_steps),
        grid_spec=grid_spec,
        out_shape=[jax.ShapeDtypeStruct((db, N_HEADS, KV_RANK), F32),
                   jax.ShapeDtypeStruct((batch * seq, MLA_WIDTH), BF16)],
        compiler_params=pltpu.CompilerParams(
            dimension_semantics=("arbitrary",), vmem_limit_bytes=VMEM_LIMIT),
        name="paged_decode_prompt_attn",
    )(page_table, qcat, c_own, k_own, cache_lat, cache_krt, q_x, k_x, v_x, k_meta, v_meta)


def _rot_cols(w):
    half = w.shape[-1] // 2
    return jnp.concatenate([-w[..., half:], w[..., :half]], axis=-1)


def _rope_tables(pos):
    inv = ROPE_THETA ** (-np.arange(0, QK_ROPE, 2, dtype=np.float64) / QK_ROPE)
    ang = np.asarray(pos, np.float64)[:, None] * inv[None, :]
    ang = np.concatenate([ang, ang], axis=-1)
    z = np.zeros_like(ang)
    return (jnp.asarray(np.concatenate([np.cos(ang), z], axis=-1), F32),
            jnp.asarray(np.concatenate([np.sin(ang), z], axis=-1), F32))


def kernel(x_prompt, x_sample, cache_kv_latent, cache_k_rope, state_pool, page_table, meta_tokens,
           g_attn, w_in, g_q, w_q_b, g_kv, w_kv_b, w_pool, pool_scale, w_o, g_mlp, w_up, w_down,
           g_final):
    assert g_attn.shape[0] == 1, "single-layer trunk"
    batch, seq, _ = x_prompt.shape
    db, s1, _ = x_sample.shape
    assert s1 == 1
    assert seq >= POOL_HIST

    kr_lo = Q_RANK + KV_RANK
    win = w_in[0]
    win_r = jnp.concatenate([win[:, :kr_lo + QK_ROPE], _rot_cols(win[:, kr_lo:kr_lo + QK_ROPE]),
                             win[:, kr_lo + QK_ROPE:]], axis=1).astype(BF16)
    wq = w_q_b[0].reshape(Q_RANK, N_HEADS, QK_NOPE + QK_ROPE)
    wq_r = jnp.concatenate([wq, _rot_cols(wq[..., QK_NOPE:])], axis=-1)
    wq_r = wq_r.reshape(Q_RANK, N_HEADS * HEAD_PAD).astype(BF16)
    wk = w_kv_b[0][..., :QK_NOPE].reshape(KV_RANK, MLA_WIDTH).astype(BF16)
    wv = w_kv_b[0][..., QK_NOPE:].reshape(KV_RANK, MLA_WIDTH).astype(BF16)
    wpool = w_pool[0].astype(BF16)
    ga, gq, gkv, gm = g_attn, g_q, g_kv, g_mlp
    gf = g_final[None, :]
    pscale = pool_scale

    mix_w = (ga, win_r, gq, wq_r, gkv, wk, wv)
    x = x_prompt.reshape(batch * seq, D_MODEL)
    cos_x, sin_x = _rope_tables(N_META + np.arange(seq))
    lat_x, kpe_x, pool_x, q_x, k_x, v_x, wo, wu = _mixer_in(
        x, cos_x, sin_x, *mix_w, tm=TM_MIX, cast=(w_o[0], w_up[0]))
    xs = x_sample.reshape(db, D_MODEL)
    x_sm = jnp.concatenate([xs, meta_tokens], axis=0)
    cos_sm, sin_sm = _rope_tables(np.concatenate([np.full(db, PAST_LEN), np.arange(N_META)]))
    lat_sm, kpe_sm, pool_sm, q_sm, k_sm, v_sm = _mixer_in(x_sm, cos_sm, sin_sm, *mix_w,
                                                          tm=db + N_META)
    lat_s, kpe_s, pool_s = lat_sm[:db], kpe_sm[:db], pool_sm[:db]
    lat_m, kpe_m, pool_m = lat_sm[db:], kpe_sm[db:], pool_sm[db:]

    qcat = _absorb(q_sm, wk, db)
    n_phys = cache_kv_latent.shape[1]
    dec, o_x = _decode_attend(
        page_table, qcat, lat_s.reshape(db, 1, KV_RANK), kpe_s.reshape(db, 1, LANES),
        cache_kv_latent.reshape(n_phys, PAGE_SIZE, KV_RANK),
        jnp.swapaxes(cache_k_rope[0], 1, 2),
        q_x, k_x, v_x, k_sm[:, db:], v_sm[:, db:], batch, seq)

    h_x, hn_x, wd = _mix_out(x, o_x, pool_x, pool_m, wpool, pscale, wo, gm, w_down[0],
                             tm=TM_OUT, seq=seq)
    y_x = _mlp(h_x, hn_x, wu, wd, gf, tm=TM_MLP, tf=TF_MLP)

    per_seq = lambda a: a.reshape(batch, seq, a.shape[-1])
    with_meta = lambda m, a: jnp.concatenate(
        [jnp.broadcast_to(m[None], (batch,) + m.shape), per_seq(a)], axis=1)[None]
    y_prompt = per_seq(y_x)
    new_lat_prompt = with_meta(lat_m, lat_x)
    new_krope_prompt = with_meta(kpe_m[:, :QK_ROPE], kpe_x[:, :QK_ROPE])
    new_pool_prompt = per_seq(pool_x)[None, :, seq - POOL_HIST:]

    state = state_pool[0]
    h_s, hn_s = _mix_out_sample(xs, dec, wv, pool_s, state, wpool, pscale, wo, gm)
    y_s = _mlp(h_s, hn_s, wu, wd, gf, tm=db, tf=TF_MLP)

    y_sample = y_s.reshape(db, 1, D_MODEL)
    new_lat_sample = lat_s.reshape(1, db, 1, KV_RANK)
    new_krope_sample = kpe_s[:, :QK_ROPE].reshape(1, db, 1, QK_ROPE)
    new_pool_sample = jnp.concatenate([state[:, 1:], pool_s[:, None, :]], axis=1)[None]
    return (y_prompt, y_sample, new_lat_prompt, new_krope_prompt, new_pool_prompt,
            new_lat_sample, new_krope_sample, new_pool_sample)
```

```python
from functools import partial

import jax
import jax.numpy as jnp
import numpy as np
from jax import lax
from jax.experimental import pallas as pl
from jax.experimental.pallas import tpu as pltpu

D_MODEL = 2048
N_META = 16
N_HEADS = 12
QK_NOPE = 128
QK_ROPE = 64
V_DIM = 128
Q_RANK = 512
KV_RANK = 512
MLA_WIDTH = N_HEADS * V_DIM
POOL_WIDTH = D_MODEL - MLA_WIDTH
POOL_WINDOWS = (2, 4, 8, 16)
POOL_GROUP = 128
POOL_HIST = 15
D_FF = 4 * D_MODEL
ROPE_THETA = 10000.0
NORM_EPS = 1e-6
PAST_LEN = 16384
PAGE_SIZE = 128

LANES = 128
HEAD_PAD = 256
QCAT = 640
IN_COLS_R = Q_RANK + KV_RANK + 2 * QK_ROPE + POOL_WIDTH
SCALE = (QK_NOPE + QK_ROPE) ** -0.5
VMEM_LIMIT = 56 * 1024 * 1024
TM_MIX = 256
TM_OUT = 512
TQ_ATTN = 256
TM_MLP = 512
TF_MLP = 1024
DECODE_GROUP_PAGES = 16
DECODE_SLOTS = 8
DECODE_WAIT_GROUPS = 4

BF16 = jnp.bfloat16
F32 = jnp.float32


def _rms(x, g):
    return x * lax.rsqrt(jnp.mean(x * x, axis=-1, keepdims=True) + NORM_EPS) * g


def _rope128(x, c, s):
    return x * c + pltpu.roll(x, QK_ROPE, axis=1) * s


def _const_spec(shape):
    n = len(shape)
    return pl.BlockSpec(shape, lambda *_: (0,) * n, pipeline_mode=pl.Buffered(1))


def _mixer_in_kernel(x_ref, ga_ref, win_ref, gq_ref, wq_ref, gkv_ref, wk_ref, wv_ref,
                     c_ref, s_ref, *rest, n_cast):
    cast_in, rest = rest[:n_cast], rest[n_cast:]
    lat_ref, kpe_ref, pool_ref, q_ref, k_ref, v_ref = rest[:6]
    for w32_ref, w16_ref in zip(cast_in, rest[6:]):
        w16_ref[...] = w32_ref[...].astype(BF16)

    h = _rms(x_ref[...], ga_ref[...]).astype(BF16)
    proj = jnp.dot(h, win_ref[...], preferred_element_type=F32)
    cos = c_ref[...]
    sin = s_ref[...]
    ckv = _rms(proj[:, Q_RANK:Q_RANK + KV_RANK], gkv_ref[...])
    lat_ref[...] = ckv
    kpe = _rope128(proj[:, 2 * Q_RANK:2 * Q_RANK + LANES], cos, sin)
    kpe_ref[...] = kpe
    pool_ref[...] = proj[:, 2 * Q_RANK + LANES:]
    kpe_b = kpe.astype(BF16)

    qn = _rms(proj[:, :Q_RANK], gq_ref[...]).astype(BF16)
    q = jnp.dot(qn, wq_ref[...], preferred_element_type=F32)
    for hh in range(N_HEADS):
        base = hh * HEAD_PAD
        q_ref[hh, :, :LANES] = (q[:, base:base + LANES] * SCALE).astype(BF16)
        pe = _rope128(q[:, base + LANES:base + HEAD_PAD], cos, sin) * SCALE
        q_ref[hh, :, LANES:] = pe.astype(BF16)

    cb = ckv.astype(BF16)
    kn = jnp.dot(cb, wk_ref[...], preferred_element_type=F32)
    vv = jnp.dot(cb, wv_ref[...], preferred_element_type=F32)
    for hh in range(N_HEADS):
        k_ref[hh, :, :LANES] = kn[:, hh * LANES:(hh + 1) * LANES].astype(BF16)
        k_ref[hh, :, LANES:] = kpe_b
        v_ref[hh] = vv[:, hh * LANES:(hh + 1) * LANES].astype(BF16)


def _mixer_in(x, cos, sin, ga, win, gq, wq, gkv, wk, wv, tm, cast=()):
    t = x.shape[0]
    steps = t // tm
    table_blocks = cos.shape[0] // tm
    assert t % tm == 0 and cos.shape[0] % tm == 0
    assert all(w.shape[0] % steps == 0 for w in cast)
    row = lambda w: pl.BlockSpec((tm, w), lambda i: (i, 0))
    table = pl.BlockSpec((tm, LANES), lambda i: (i % table_blocks, 0))
    head = lambda w: pl.BlockSpec((N_HEADS, tm, w), lambda i: (0, i, 0))
    cast_specs = [pl.BlockSpec((w.shape[0] // steps, w.shape[1]), lambda i: (i, 0)) for w in cast]
    return pl.pallas_call(
        partial(_mixer_in_kernel, n_cast=len(cast)),
        grid=(steps,),
        in_specs=[row(D_MODEL), _const_spec((1, D_MODEL)), _const_spec((D_MODEL, IN_COLS_R)),
                  _const_spec((1, Q_RANK)), _const_spec((Q_RANK, N_HEADS * HEAD_PAD)),
                  _const_spec((1, KV_RANK)), _const_spec((KV_RANK, MLA_WIDTH)),
                  _const_spec((KV_RANK, MLA_WIDTH)), table, table] + cast_specs,
        out_specs=[row(KV_RANK), row(LANES), row(POOL_WIDTH),
                   head(HEAD_PAD), head(HEAD_PAD), head(V_DIM)] + cast_specs,
        out_shape=[jax.ShapeDtypeStruct((t, KV_RANK), F32),
                   jax.ShapeDtypeStruct((t, LANES), F32),
                   jax.ShapeDtypeStruct((t, POOL_WIDTH), F32),
                   jax.ShapeDtypeStruct((N_HEADS, t, HEAD_PAD), BF16),
                   jax.ShapeDtypeStruct((N_HEADS, t, HEAD_PAD), BF16),
                   jax.ShapeDtypeStruct((N_HEADS, t, V_DIM), BF16)]
                  + [jax.ShapeDtypeStruct(w.shape, BF16) for w in cast],
        compiler_params=pltpu.CompilerParams(
            dimension_semantics=("arbitrary",), vmem_limit_bytes=VMEM_LIMIT),
        name="mixer_in",
    )(x, ga, win, gq, wq, gkv, wk, wv, cos, sin, *cast)


_NT = (((1,), (1,)), ((), ()))


def _attn_tiles(q_ref, k_ref, v_ref, km_ref, vm_ref, o_ref, tiles, tq):
    km = km_ref[0]
    vm = vm_ref[0]
    row = lax.broadcasted_iota(jnp.int32, (tq, tq), 0)
    col = lax.broadcasted_iota(jnp.int32, (tq, tq), 1)
    for ti in tiles:
        r0 = ti * tq
        q = q_ref[0, r0:r0 + tq, :]
        s_meta = lax.dot_general(q, km, _NT, preferred_element_type=F32)
        s_diag = lax.dot_general(q, k_ref[0, r0:r0 + tq, :], _NT, preferred_element_type=F32)
        s_diag = jnp.where(col <= row, s_diag, -jnp.inf)
        m = jnp.maximum(jnp.max(s_diag, axis=-1, keepdims=True),
                        jnp.max(s_meta, axis=-1, keepdims=True))
        if r0:
            s_left = lax.dot_general(q, k_ref[0, :r0, :], _NT, preferred_element_type=F32)
            m = jnp.maximum(m, jnp.max(s_left, axis=-1, keepdims=True))
            p_left = jnp.exp(s_left - m)
        p_diag = jnp.exp(s_diag - m)
        p_meta = jnp.exp(s_meta - m)
        l = jnp.sum(p_diag, axis=-1, keepdims=True) + jnp.sum(p_meta, axis=-1, keepdims=True)
        o = (jnp.dot(p_diag.astype(BF16), v_ref[0, r0:r0 + tq, :], preferred_element_type=F32)
             + jnp.dot(p_meta.astype(BF16), vm, preferred_element_type=F32))
        if r0:
            l = l + jnp.sum(p_left, axis=-1, keepdims=True)
            o = o + jnp.dot(p_left.astype(BF16), v_ref[0, :r0, :], preferred_element_type=F32)
        o_ref[r0:r0 + tq, :] = (o / l).astype(BF16)


def _balanced_halves(n_tiles):
    assert n_tiles % 4 == 0
    pairs = [(i, n_tiles - 1 - i) for i in range(n_tiles // 2)]
    return (tuple(t for p in pairs[0::2] for t in p), tuple(t for p in pairs[1::2] for t in p))


def _mix_project(x, attn_o, mixed_fn, wpool_ref, pscale_ref, wo_ref, gm_ref, h_ref, hn_ref):
    parts = [attn_o]
    for g in range(len(POOL_WINDOWS)):
        lo = g * POOL_GROUP
        y = jnp.dot(mixed_fn(g).astype(BF16), wpool_ref[g], preferred_element_type=F32)
        parts.append((y * pscale_ref[:, lo:lo + POOL_GROUP]).astype(BF16))
    h = x + jnp.dot(jnp.concatenate(parts, axis=1), wo_ref[...], preferred_element_type=F32)
    h_ref[...] = h
    hn_ref[...] = _rms(h, gm_ref[...]).astype(BF16)


def _mix_out_kernel(x_ref, o_ref, pool_ref, halo_ref, pmeta_ref, wpool_ref, pscale_ref, wo_ref,
                    gm_ref, w32_ref, h_ref, hn_ref, w16_ref, ext_ref, *, tm, tiles_per_seq):
    w16_ref[...] = w32_ref[...].astype(BF16)

    i = pl.program_id(0)
    first = (i % tiles_per_seq) == 0
    ext_ref[0:N_META, :] = jnp.where(first, pmeta_ref[...], halo_ref[...])
    ext_ref[N_META:, :] = pool_ref[...]

    def mixed(g):
        w = POOL_WINDOWS[g]
        lo = g * POOL_GROUP
        cur = ext_ref[N_META:N_META + tm, lo:lo + POOL_GROUP]
        tot = cur
        for kk in range(1, w):
            tot = tot + ext_ref[N_META - kk:N_META - kk + tm, lo:lo + POOL_GROUP]
        return tot / float(w) - cur

    _mix_project(x_ref[...], o_ref[...], mixed, wpool_ref, pscale_ref, wo_ref, gm_ref,
                 h_ref, hn_ref)


def _mix_out(x, o, pool, pool_meta, wpool, pscale, wo, gm, w_cast, tm, seq):
    t = x.shape[0]
    steps = t // tm
    assert seq % tm == 0 and tm % N_META == 0 and N_META > POOL_HIST
    assert w_cast.shape[0] % steps == 0
    row = lambda w: pl.BlockSpec((tm, w), lambda i: (i, 0))
    cast_spec = pl.BlockSpec((w_cast.shape[0] // steps, w_cast.shape[1]), lambda i: (i, 0))
    hb = tm // N_META
    return pl.pallas_call(
        partial(_mix_out_kernel, tm=tm, tiles_per_seq=seq // tm),
        grid=(steps,),
        in_specs=[row(D_MODEL), row(MLA_WIDTH), row(POOL_WIDTH),
                  pl.BlockSpec((N_META, POOL_WIDTH), lambda i: (jnp.maximum(i * hb - 1, 0), 0)),
                  _const_spec((N_META, POOL_WIDTH)),
                  _const_spec((len(POOL_WINDOWS), POOL_GROUP, POOL_GROUP)),
                  _const_spec((1, POOL_WIDTH)), _const_spec((D_MODEL, D_MODEL)),
                  _const_spec((1, D_MODEL)), cast_spec],
        out_specs=[row(D_MODEL), row(D_MODEL), cast_spec],
        out_shape=[jax.ShapeDtypeStruct((t, D_MODEL), F32),
                   jax.ShapeDtypeStruct((t, D_MODEL), BF16),
                   jax.ShapeDtypeStruct(w_cast.shape, BF16)],
        scratch_shapes=[pltpu.VMEM((tm + N_META, POOL_WIDTH), F32)],
        compiler_params=pltpu.CompilerParams(
            dimension_semantics=("arbitrary",), vmem_limit_bytes=VMEM_LIMIT),
        name="mix_out",
    )(x, o, pool, pool, pool_meta, wpool, pscale, wo, gm, w_cast)


def _mix_out_sample_kernel(x_ref, lat_ref, wv_ref, pin_ref, st_ref, wpool_ref, pscale_ref, wo_ref,
                           gm_ref, h_ref, hn_ref):
    attn_o = jnp.concatenate(
        [jnp.dot(lat_ref[:, hh, :].astype(BF16), wv_ref[:, hh * V_DIM:(hh + 1) * V_DIM],
                 preferred_element_type=F32).astype(BF16) for hh in range(N_HEADS)], axis=1)

    def mixed(g):
        w = POOL_WINDOWS[g]
        lo = g * POOL_GROUP
        cur = pin_ref[:, lo:lo + POOL_GROUP]
        tot = cur
        for kk in range(1, w):
            tot = tot + st_ref[:, POOL_HIST - kk, lo:lo + POOL_GROUP]
        return tot / float(w) - cur

    _mix_project(x_ref[...], attn_o, mixed, wpool_ref, pscale_ref, wo_ref, gm_ref, h_ref, hn_ref)


def _mix_out_sample(x, lat, wv, pin, state, wpool, pscale, wo, gm):
    t = x.shape[0]
    full = lambda a: pl.BlockSpec(a.shape, lambda: (0,) * a.ndim)
    args = (x, lat, wv, pin, state, wpool, pscale, wo, gm)
    return pl.pallas_call(
        _mix_out_sample_kernel,
        in_specs=[full(a) for a in args],
        out_specs=[pl.BlockSpec((t, D_MODEL), lambda: (0, 0))] * 2,
        out_shape=[jax.ShapeDtypeStruct((t, D_MODEL), F32),
                   jax.ShapeDtypeStruct((t, D_MODEL), BF16)],
        compiler_params=pltpu.CompilerParams(vmem_limit_bytes=VMEM_LIMIT),
        name="mix_out_sample",
    )(*args)


def _mlp_kernel(h_ref, hn_ref, wu_ref, wd_ref, gf_ref, y_ref):
    f = pl.program_id(1)

    @pl.when(f == 0)
    def _():
        y_ref[...] = h_ref[...]

    z = jnp.dot(hn_ref[...], wu_ref[...], preferred_element_type=F32)
    a = jnp.square(jnp.maximum(z, 0.0)).astype(BF16)
    y_ref[...] += jnp.dot(a, wd_ref[...], preferred_element_type=F32)

    @pl.when(f == pl.num_programs(1) - 1)
    def _():
        y_ref[...] = _rms(y_ref[...], gf_ref[...])


def _mlp(h, hn, wu, wd, gf, tm, tf):
    t = h.shape[0]
    assert t % tm == 0 and D_FF % tf == 0
    return pl.pallas_call(
        _mlp_kernel,
        grid=(t // tm, D_FF // tf),
        in_specs=[pl.BlockSpec((tm, D_MODEL), lambda i, f: (i, 0)),
                  pl.BlockSpec((tm, D_MODEL), lambda i, f: (i, 0)),
                  pl.BlockSpec((D_MODEL, tf), lambda i, f: (0, f)),
                  pl.BlockSpec((tf, D_MODEL), lambda i, f: (f, 0)),
                  pl.BlockSpec((1, D_MODEL), lambda i, f: (0, 0))],
        out_specs=pl.BlockSpec((tm, D_MODEL), lambda i, f: (i, 0)),
        out_shape=jax.ShapeDtypeStruct((t, D_MODEL), F32),
        compiler_params=pltpu.CompilerParams(
            dimension_semantics=("arbitrary", "arbitrary"), vmem_limit_bytes=VMEM_LIMIT),
        name="mlp",
    )(h, hn, wu, wd, gf)


def _absorb_kernel(q_ref, wk_ref, o_ref, *, db):
    for hh in range(N_HEADS):
        q = q_ref[hh, :db, :]
        qlat = lax.dot_general(q[:, :LANES], wk_ref[:, hh * LANES:(hh + 1) * LANES], _NT,
                               preferred_element_type=F32)
        o_ref[:, hh, :KV_RANK] = qlat
        o_ref[:, hh, KV_RANK:] = q[:, LANES:].astype(F32)


def _absorb(q, wk, db):
    full = lambda a: pl.BlockSpec(a.shape, lambda: (0,) * a.ndim)
    return pl.pallas_call(
        partial(_absorb_kernel, db=db),
        in_specs=[full(q), full(wk)],
        out_specs=pl.BlockSpec((db, N_HEADS, QCAT), lambda: (0, 0, 0)),
        out_shape=jax.ShapeDtypeStruct((db, N_HEADS, QCAT), F32),
        compiler_params=pltpu.CompilerParams(vmem_limit_bytes=VMEM_LIMIT),
        name="absorb_q",
    )(q, wk)


def _decode_attend_kernel(pt_ref, q_ref, cown_ref, kown_ref, lat_hbm, krt_hbm,
                          qa_ref, ka_ref, va_ref, km_ref, vm_ref, o_ref, oa_ref,
                          latbuf, krtbuf, sem, *, gp, ng, nslot, wchunk, tq, halves, attn_steps):
    b = pl.program_id(0)
    nb = pl.num_programs(0)

    def page_copies(bb, gi, k):
        slot = gi % nslot
        page = pt_ref[bb, gi * gp + k]
        return (pltpu.make_async_copy(lat_hbm.at[page], latbuf.at[slot, k], sem.at[slot]),
                pltpu.make_async_copy(krt_hbm.at[page], krtbuf.at[slot, k], sem.at[slot]))

    def start(bb, gi):
        for k in range(gp):
            for cp in page_copies(bb, gi, k):
                cp.start(priority=k % 2)

    def wait(bb, gi):
        for k in range(gp):
            for cp in page_copies(bb, gi, k):
                cp.wait()

    def refill_target(gi):
        nxt = gi + nslot
        if nxt < ng:
            return b, nxt
        return jnp.minimum(b + 1, nb - 1), nxt - ng

    @pl.when(b == 0)
    def _():
        for gi in range(nslot):
            start(0, gi)

    q = q_ref[0]
    qlat = q[:, :KV_RANK]
    qpe = q[:, KV_RANK:KV_RANK + QK_ROPE]

    def scores(gi):
        slot = gi % nslot
        parts = [lax.dot_general(qlat, latbuf[slot, k], _NT, preferred_element_type=F32)
                 + jnp.dot(qpe, krtbuf[slot, k], preferred_element_type=F32)
                 for k in range(gp)]
        return jnp.concatenate(parts, axis=-1)

    def fold(gi, s, m, l, acc):
        slot = gi % nslot
        m_new = jnp.maximum(m, jnp.max(s, axis=-1, keepdims=True))
        corr = jnp.exp(m - m_new)
        p = jnp.exp(s - m_new)
        l = l * corr + jnp.sum(p, axis=-1, keepdims=True)
        acc = acc * corr
        for k in range(gp):
            acc = acc + jnp.dot(p[:, k * PAGE_SIZE:(k + 1) * PAGE_SIZE], latbuf[slot, k],
                                preferred_element_type=F32)
        start(*refill_target(gi))
        return m_new, l, acc

    c_own = cown_ref[0]
    k_own = kown_ref[0]
    m = (jnp.sum(qlat * c_own, axis=-1, keepdims=True)
         + jnp.sum(qpe * k_own[:, :QK_ROPE], axis=-1, keepdims=True))
    l = jnp.ones_like(m)
    acc = jnp.broadcast_to(c_own, (N_HEADS, KV_RANK))

    for gw in range(wchunk):
        wait(b, gw)
    s_cur = scores(0)
    for gi in range(ng):
        if gi + 1 < ng:
            if (gi + 1) % wchunk == 0:
                for gw in range(gi + 1, gi + 1 + wchunk):
                    wait(b, gw)
            s_next = scores(gi + 1)
        m, l, acc = fold(gi, s_cur, m, l, acc)
        if gi + 1 < ng:
            s_cur = s_next
    o_ref[0] = acc / l

    for half, tiles in enumerate(halves):
        @pl.when((b < attn_steps) & (b % len(halves) == half))
        def _(tiles=tiles):
            _attn_tiles(qa_ref, ka_ref, va_ref, km_ref, vm_ref, oa_ref, tiles, tq)

    @pl.when(b == nb - 1)
    def _():
        for gi in range(nslot):
            wait(nb - 1, gi)


def _decode_attend(page_table, qcat, c_own, k_own, cache_lat, cache_krt,
                   q_x, k_x, v_x, k_meta, v_meta, batch, seq):
    db, n_pages = page_table.shape
    gp, nslot, wchunk, tq = DECODE_GROUP_PAGES, DECODE_SLOTS, DECODE_WAIT_GROUPS, TQ_ATTN
    ng = n_pages // gp
    assert n_pages % gp == 0 and ng % nslot == 0 and seq % tq == 0
    assert ng % wchunk == 0 and wchunk < nslot
    n_units = batch * N_HEADS
    halves = _balanced_halves(seq // tq)
    attn_steps = len(halves) * n_units
    assert attn_steps <= db

    def unit(b):
        u = jnp.minimum(b // len(halves), n_units - 1)
        return u % N_HEADS, u // N_HEADS

    per_seq = lambda shape: pl.BlockSpec(shape, lambda b, pt: (b, 0, 0))
    a_seq = lambda w: pl.BlockSpec((1, seq, w), lambda b, pt: (*unit(b), 0))
    a_meta = lambda w: pl.BlockSpec((1, N_META, w), lambda b, pt: (unit(b)[0], 0, 0))
    grid_spec = pltpu.PrefetchScalarGridSpec(
        num_scalar_prefetch=1,
        grid=(db,),
        in_specs=[per_seq((1, N_HEADS, QCAT)), per_seq((1, 1, KV_RANK)), per_seq((1, 1, LANES)),
                  pl.BlockSpec(memory_space=pl.ANY), pl.BlockSpec(memory_space=pl.ANY),
                  a_seq(HEAD_PAD), a_seq(HEAD_PAD), a_seq(V_DIM), a_meta(HEAD_PAD), a_meta(V_DIM)],
        out_specs=[per_seq((1, N_HEADS, KV_RANK)),
                   pl.BlockSpec((seq, V_DIM), lambda b, pt: unit(b)[::-1])],
        scratch_shapes=[pltpu.VMEM((nslot, gp, PAGE_SIZE, KV_RANK), F32),
                        pltpu.VMEM((nslot, gp, QK_ROPE, PAGE_SIZE), F32),
                        pltpu.SemaphoreType.DMA((nslot,))],
    )
    return pl.pallas_call(
        partial(_decode_attend_kernel, gp=gp, ng=ng, nslot=nslot, wchunk=wchunk, tq=tq, halves=halves,
                attn_steps=attn_steps),
        grid_spec=grid_spec,
        out_shape=[jax.ShapeDtypeStruct((db, N_HEADS, KV_RANK), F32),
                   jax.ShapeDtypeStruct((batch * seq, MLA_WIDTH), BF16)],
        compiler_params=pltpu.CompilerParams(
            dimension_semantics=("arbitrary",), vmem_limit_bytes=VMEM_LIMIT),
        name="paged_decode_prompt_attn",
    )(page_table, qcat, c_own, k_own, cache_lat, cache_krt, q_x, k_x, v_x, k_meta, v_meta)


def _rot_cols(w):
    half = w.shape[-1] // 2
    return jnp.concatenate([-w[..., half:], w[..., :half]], axis=-1)


def _rope_tables(pos):
    inv = ROPE_THETA ** (-np.arange(0, QK_ROPE, 2, dtype=np.float64) / QK_ROPE)
    ang = np.asarray(pos, np.float64)[:, None] * inv[None, :]
    ang = np.concatenate([ang, ang], axis=-1)
    z = np.zeros_like(ang)
    return (jnp.asarray(np.concatenate([np.cos(ang), z], axis=-1), F32),
            jnp.asarray(np.concatenate([np.sin(ang), z], axis=-1), F32))


def kernel(x_prompt, x_sample, cache_kv_latent, cache_k_rope, state_pool, page_table, meta_tokens,
           g_attn, w_in, g_q, w_q_b, g_kv, w_kv_b, w_pool, pool_scale, w_o, g_mlp, w_up, w_down,
           g_final):
    assert g_attn.shape[0] == 1, "single-layer trunk"
    batch, seq, _ = x_prompt.shape
    db, s1, _ = x_sample.shape
    assert s1 == 1
    assert seq >= POOL_HIST

    kr_lo = Q_RANK + KV_RANK
    win = w_in[0]
    win_r = jnp.concatenate([win[:, :kr_lo + QK_ROPE], _rot_cols(win[:, kr_lo:kr_lo + QK_ROPE]),
                             win[:, kr_lo + QK_ROPE:]], axis=1).astype(BF16)
    wq = w_q_b[0].reshape(Q_RANK, N_HEADS, QK_NOPE + QK_ROPE)
    wq_r = jnp.concatenate([wq, _rot_cols(wq[..., QK_NOPE:])], axis=-1)
    wq_r = wq_r.reshape(Q_RANK, N_HEADS * HEAD_PAD).astype(BF16)
    wk = w_kv_b[0][..., :QK_NOPE].reshape(KV_RANK, MLA_WIDTH).astype(BF16)
    wv = w_kv_b[0][..., QK_NOPE:].reshape(KV_RANK, MLA_WIDTH).astype(BF16)
    wpool = w_pool[0].astype(BF16)
    ga, gq, gkv, gm = g_attn, g_q, g_kv, g_mlp
    gf = g_final[None, :]
    pscale = pool_scale

    mix_w = (ga, win_r, gq, wq_r, gkv, wk, wv)
    x = x_prompt.reshape(batch * seq, D_MODEL)
    cos_x, sin_x = _rope_tables(N_META + np.arange(seq))
    lat_x, kpe_x, pool_x, q_x, k_x, v_x, wo, wu = _mixer_in(
        x, cos_x, sin_x, *mix_w, tm=TM_MIX, cast=(w_o[0], w_up[0]))
    xs = x_sample.reshape(db, D_MODEL)
    x_sm = jnp.concatenate([xs, meta_tokens], axis=0)
    cos_sm, sin_sm = _rope_tables(np.concatenate([np.full(db, PAST_LEN), np.arange(N_META)]))
    lat_sm, kpe_sm, pool_sm, q_sm, k_sm, v_sm = _mixer_in(x_sm, cos_sm, sin_sm, *mix_w,
                                                          tm=db + N_META)
    lat_s, kpe_s, pool_s = lat_sm[:db], kpe_sm[:db], pool_sm[:db]
    lat_m, kpe_m, pool_m = lat_sm[db:], kpe_sm[db:], pool_sm[db:]

    qcat = _absorb(q_sm, wk, db)
    n_phys = cache_kv_latent.shape[1]
    dec, o_x = _decode_attend(
        page_table, qcat, lat_s.reshape(db, 1, KV_RANK), kpe_s.reshape(db, 1, LANES),
        cache_kv_latent.reshape(n_phys, PAGE_SIZE, KV_RANK),
        jnp.swapaxes(cache_k_rope[0], 1, 2),
        q_x, k_x, v_x, k_sm[:, db:], v_sm[:, db:], batch, seq)

    h_x, hn_x, wd = _mix_out(x, o_x, pool_x, pool_m, wpool, pscale, wo, gm, w_down[0],
                             tm=TM_OUT, seq=seq)
    y_x = _mlp(h_x, hn_x, wu, wd, gf, tm=TM_MLP, tf=TF_MLP)

    per_seq = lambda a: a.reshape(batch, seq, a.shape[-1])
    with_meta = lambda m, a: jnp.concatenate(
        [jnp.broadcast_to(m[None], (batch,) + m.shape), per_seq(a)], axis=1)[None]
    y_prompt = per_seq(y_x)
    new_lat_prompt = with_meta(lat_m, lat_x)
    new_krope_prompt = with_meta(kpe_m[:, :QK_ROPE], kpe_x[:, :QK_ROPE])
    new_pool_prompt = per_seq(pool_x)[None, :, seq - POOL_HIST:]

    state = state_pool[0]
    h_s, hn_s = _mix_out_sample(xs, dec, wv, pool_s, state, wpool, pscale, wo, gm)
    y_s = _mlp(h_s, hn_s, wu, wd, gf, tm=db, tf=TF_MLP)

    y_sample = y_s.reshape(db, 1, D_MODEL)
    new_lat_sample = lat_s.reshape(1, db, 1, KV_RANK)
    new_krope_sample = kpe_s[:, :QK_ROPE].reshape(1, db, 1, QK_ROPE)
    new_pool_sample = jnp.concatenate([state[:, 1:], pool_s[:, None, :]], axis=1)[None]
    return (y_prompt, y_sample, new_lat_prompt, new_krope_prompt, new_pool_prompt,
            new_lat_sample, new_krope_sample, new_pool_sample)
```
